```python
import jax, jax.numpy as jnp
from jax import lax
import numpy as np

D_MODEL = 1024
BATCH = 8
SEQ = 8192
DEPTH = 2

GRID_W = 64
CTX_LEN = 256
NORM_EPS = 1e-6
ROPE_BASE = 10000.0
NEG_INF = -1e30

A_HEADS = 8
A_KV_HEADS = 2
A_HEAD_DIM = 64
A_WINDOW = 128
A_BLOCK = 128

B_HEADS = 4
B_HEAD_DIM = 128
B_CONV = 3
B_CHUNK = 64

C_HEADS = 16
C_NOPE = 64
C_ROPE = 32
C_V = 64
C_Q_RANK = 384
C_KV_RANK = 256
C_QBLOCK = 128

FFN_HIDDEN = ((8 * D_MODEL + 3 * 256 - 1) // (3 * 256)) * 256

A_Q_W = A_HEADS * A_HEAD_DIM
A_KV_W = A_KV_HEADS * A_HEAD_DIM
B_W = B_HEADS * B_HEAD_DIM
AB_SIZES = (A_Q_W, A_KV_W, A_KV_W, 3 * B_W, B_W, 2 * B_HEADS, 2 * B_HEADS)
AB_IN = sum(AB_SIZES)
AB_SPLITS = tuple(int(s) for s in np.cumsum(AB_SIZES)[:-1])
AB_OUT = A_Q_W + B_W
C_IN = C_Q_RANK + C_KV_RANK + C_ROPE
C_SPLITS = (C_Q_RANK, C_Q_RANK + C_KV_RANK)
C_OUT = C_HEADS * C_V

kernel_name = "hybrid_dit_swa_gdn_mla"


def rms_norm(x, gain):
    xf = x.astype(jnp.float32)
    y = xf * lax.rsqrt(jnp.mean(xf * xf, axis=-1, keepdims=True) + NORM_EPS)
    return (y * gain.astype(jnp.float32)).astype(x.dtype)


def l2_normalize(x):
    return x * lax.rsqrt(jnp.sum(x * x, axis=-1, keepdims=True) + NORM_EPS)


def modulate(x, gain, shift, scale):
    return rms_norm(x, gain) * (1 + scale) + shift


def adaln(cond, w, b):
    return jax.nn.silu(cond) @ w + b


def swiglu(u, w_gate, w_up, w_down):
    return (jax.nn.silu(u @ w_gate) * (u @ w_up)) @ w_down


def flip_seq(t):
    return t[:, ::-1]


def axial_rope_tables(rows, rot_dim):
    n_freq = rot_dim // 4
    inv_freq = ROPE_BASE ** (-jnp.arange(n_freq, dtype=jnp.float32) / n_freq)
    row = jnp.repeat(jnp.arange(rows, dtype=jnp.float32), GRID_W)
    col = jnp.tile(jnp.arange(GRID_W, dtype=jnp.float32), rows)
    ang = jnp.concatenate([row[:, None] * inv_freq, col[:, None] * inv_freq], axis=-1)
    return jnp.cos(ang), jnp.sin(ang)


def apply_rope(x, cos, sin):
    half = x.shape[-1] // 2
    x1, x2 = x[..., :half], x[..., half:]
    cs = cos[:, None, :].astype(x.dtype)
    sn = sin[:, None, :].astype(x.dtype)
    return jnp.concatenate([x1 * cs - x2 * sn, x1 * sn + x2 * cs], axis=-1)


def softmax_with_sink(s, sink_logit):
    s_sink = jnp.broadcast_to(sink_logit, s.shape[:-1] + (1,))
    return jax.nn.softmax(jnp.concatenate([s, s_sink], axis=-1), axis=-1)[..., :-1]


def window_attention(q, k, v, kc, vc, sink_logit):
    B, S, H, d = q.shape
    Hkv = k.shape[2]
    G = H // Hkv
    nb = S // A_BLOCK
    scale = d ** -0.5

    def band(t):
        tp = jnp.pad(t, ((0, 0), (A_BLOCK, A_BLOCK), (0, 0), (0, 0))).reshape(B, nb + 2, A_BLOCK, Hkv, d)
        tb = jnp.concatenate([tp[:, :-2], tp[:, 1:-1], tp[:, 2:]], axis=2)
        return jnp.moveaxis(tb, 1, 0)

    qb = jnp.moveaxis(q.reshape(B, nb, A_BLOCK, Hkv, G, d), 1, 0)
    kb, vb = band(k), band(v)
    rel = jnp.arange(3 * A_BLOCK)[None, :] - A_BLOCK - jnp.arange(A_BLOCK)[:, None]
    in_window = jnp.abs(rel) <= A_WINDOW

    def one_block(args):
        n, qn, kn, vn = args
        kpos = n * A_BLOCK - A_BLOCK + jnp.arange(3 * A_BLOCK)
        valid = in_window & ((kpos >= 0) & (kpos < S))[None, :]
        s_loc = jnp.einsum('bqhgd,bkhd->bhgqk', qn, kn).astype(jnp.float32) * scale
        s_loc = jnp.where(valid, s_loc, NEG_INF)
        s_ctx = jnp.einsum('bqhgd,bchd->bhgqc', qn, kc).astype(jnp.float32) * scale
        p = softmax_with_sink(jnp.concatenate([s_loc, s_ctx], axis=-1), sink_logit).astype(vn.dtype)
        return (jnp.einsum('bhgqk,bkhd->bqhgd', p[..., :3 * A_BLOCK], vn)
                + jnp.einsum('bhgqc,bchd->bqhgd', p[..., 3 * A_BLOCK:], vc))

    o = lax.map(one_block, (jnp.arange(nb), qb, kb, vb))
    return jnp.moveaxis(o, 0, 1).reshape(B, S, H * d)


def context_sink_attention(q, k, v, sink_logit):
    B, L, H, d = q.shape
    Hkv = k.shape[2]
    qg = q.reshape(B, L, Hkv, H // Hkv, d)
    s = jnp.einsum('bqhgd,bkhd->bhgqk', qg, k).astype(jnp.float32) * d ** -0.5
    p = softmax_with_sink(s, sink_logit).astype(v.dtype)
    return jnp.einsum('bhgqk,bkhd->bqhgd', p, v).reshape(B, L, H * d)


def depthwise_conv_centred(x, w):
    return lax.conv_general_dilated(x, w[:, None, :].astype(x.dtype), window_strides=(1,), padding='SAME',
                                    dimension_numbers=('NWC', 'WIO', 'NWC'), feature_group_count=x.shape[-1])


def delta_inputs(qkv, dec, beta, conv_w, a_log, dt_bias):
    B, L, _ = qkv.shape
    qkv = jax.nn.silu(depthwise_conv_centred(qkv, conv_w)).astype(jnp.float32)
    q, k, v = [t.reshape(B, L, B_HEADS, B_HEAD_DIM) for t in jnp.split(qkv, 3, axis=-1)]
    q = l2_normalize(q) * B_HEAD_DIM ** -0.5
    k = l2_normalize(k)
    dec = dec.astype(jnp.float32).reshape(B, L, 2, B_HEADS)
    log_decay = -jnp.exp(a_log.astype(jnp.float32)) * jax.nn.softplus(dec + dt_bias.astype(jnp.float32))
    beta = jax.nn.sigmoid(beta.astype(jnp.float32).reshape(B, L, 2, B_HEADS))
    return q, k, v, log_decay, beta


def gated_delta_rule(q, k, v, log_decay, beta, state):
    B, L, H, dk = k.shape
    dv = v.shape[-1]
    C = B_CHUNK
    nc = L // C

    def to_chunks(t):
        t = t.reshape((B, nc, C, H) + t.shape[3:])
        return jnp.moveaxis(jnp.moveaxis(t, 1, 0), 3, 2)

    qc, kc, vc, bc = to_chunks(q), to_chunks(k), to_chunks(v), to_chunks(beta)
    gc = jnp.cumsum(to_chunks(log_decay), axis=-1)
    idx = jnp.arange(C)
    lower = idx[:, None] >= idx[None, :]
    strict = idx[:, None] > idx[None, :]
    decay = jnp.exp(jnp.where(lower, gc[..., :, None] - gc[..., None, :], -jnp.inf))
    kbeta = kc * bc[..., None]
    a_kk = jnp.where(strict, jnp.einsum('...id,...jd->...ij', kbeta, kc) * decay, 0.0)
    rhs = jnp.concatenate([vc * bc[..., None], kbeta * jnp.exp(gc)[..., None]], axis=-1)
    sol = lax.linalg.triangular_solve(a_kk + jnp.eye(C, dtype=jnp.float32), rhs,
                                      left_side=True, lower=True, unit_diagonal=True)
    u, w = sol[..., :dv], sol[..., dv:]
    a_qk = jnp.einsum('...id,...jd->...ij', qc, kc) * decay
    g_last = gc[..., -1]
    k_tail = kc * jnp.exp(g_last[..., None] - gc)[..., None]
    q_head = qc * jnp.exp(gc)[..., None]

    def step(S, xs):
        u_i, w_i, qh_i, a_i, kt_i, gl_i = xs
        v_new = u_i - jnp.einsum('bhck,bhkv->bhcv', w_i, S)
        o = jnp.einsum('bhck,bhkv->bhcv', qh_i, S) + jnp.einsum('bhij,bhjv->bhiv', a_i, v_new)
        S = S * jnp.exp(gl_i)[..., None, None] + jnp.einsum('bhck,bhcv->bhkv', kt_i, v_new)
        return S, o

    S, o = lax.scan(step, state, (u, w, q_head, a_qk, k_tail, g_last))
    o = jnp.moveaxis(jnp.moveaxis(o, 0, 1), 2, 3).reshape(B, L, H, dv)
    return o, S


def gated_head_norm(o, gate, gain):
    B, L = o.shape[:2]
    y = rms_norm(o, gain) * jax.nn.silu(gate.astype(jnp.float32).reshape(o.shape))
    return y.reshape(B, L, -1).astype(gate.dtype)


def split_ab(p):
    B, L, _ = p.shape
    qa, ka, va, qkvb, gb, dec, beta = jnp.split(p, AB_SPLITS, axis=-1)
    return (qa.reshape(B, L, A_HEADS, A_HEAD_DIM), ka.reshape(B, L, A_KV_HEADS, A_HEAD_DIM),
            va.reshape(B, L, A_KV_HEADS, A_HEAD_DIM), qkvb, gb, dec, beta)


def hybrid_ab_mixer(u_lat, u_ctx, w_in, sink, conv_w, a_log, dt_bias, out_norm, w_out, cos, sin, with_ctx):
    B = u_lat.shape[0]
    qa_l, ka_l, va_l, qkv_l, gate_l, dec_l, beta_l = split_ab(u_lat @ w_in)
    qa_c, ka_c, va_c, qkv_c, gate_c, dec_c, beta_c = split_ab(u_ctx @ w_in)
    sink_logit = sink.astype(jnp.float32).reshape(A_KV_HEADS, A_HEADS // A_KV_HEADS, 1, 1)
    oa_l = window_attention(apply_rope(qa_l, cos, sin), apply_rope(ka_l, cos, sin), va_l, ka_c, va_c, sink_logit)
    q_l, k_l, v_l, g_l, b_l = delta_inputs(qkv_l, dec_l, beta_l, conv_w, a_log, dt_bias)
    q_c, k_c, v_c, g_c, b_c = delta_inputs(qkv_c, dec_c, beta_c, conv_w, a_log, dt_bias)
    zero = jnp.zeros((B, B_HEADS, B_HEAD_DIM, B_HEAD_DIM), jnp.float32)
    o_cf, s_f = gated_delta_rule(q_c, k_c, v_c, g_c[:, :, 0], b_c[:, :, 0], zero)
    o_cb, s_b = gated_delta_rule(flip_seq(q_c), flip_seq(k_c), flip_seq(v_c),
                                 flip_seq(g_c[:, :, 1]), flip_seq(b_c[:, :, 1]), zero)
    o_lf, _ = gated_delta_rule(q_l, k_l, v_l, g_l[:, :, 0], b_l[:, :, 0], s_f)
    o_lb, _ = gated_delta_rule(flip_seq(q_l), flip_seq(k_l), flip_seq(v_l),
                               flip_seq(g_l[:, :, 1]), flip_seq(b_l[:, :, 1]), s_b)
    ob_l = gated_head_norm(o_lf + flip_seq(o_lb), gate_l, out_norm)
    y_lat = jnp.concatenate([oa_l, ob_l.astype(oa_l.dtype)], axis=-1) @ w_out
    if not with_ctx:
        return y_lat, None
    oa_c = context_sink_attention(qa_c, ka_c, va_c, sink_logit)
    ob_c = gated_head_norm(o_cf + flip_seq(o_cb), gate_c, out_norm)
    y_ctx = jnp.concatenate([oa_c, ob_c.astype(oa_c.dtype)], axis=-1) @ w_out
    return y_lat, y_ctx


def mla_queries(cq, q_norm, w_qb):
    B, L, _ = cq.shape
    q = (rms_norm(cq, q_norm) @ w_qb).reshape(B, L, C_HEADS, C_NOPE + C_ROPE)
    return q[..., :C_NOPE], q[..., C_NOPE:]


def mla_keys(ckv, kv_norm, w_kvb):
    B, L, _ = ckv.shape
    kv = (rms_norm(ckv, kv_norm) @ w_kvb).reshape(B, L, C_HEADS, C_NOPE + C_V)
    return kv[..., :C_NOPE], kv[..., C_NOPE:]


def mla_attend(q_nope, q_pe, k_nope, k_pe, v):
    s = (jnp.einsum('bqhd,bkhd->bhqk', q_nope, k_nope)
         + jnp.einsum('bqhd,bkd->bhqk', q_pe, k_pe)).astype(jnp.float32) * (C_NOPE + C_ROPE) ** -0.5
    p = jax.nn.softmax(s, axis=-1).astype(v.dtype)
    return jnp.einsum('bhqk,bkhd->bqhd', p, v)


def mla_mixer(u_lat, u_ctx, w_in, q_norm, kv_norm, w_qb, w_kvb, w_out, cos, sin, with_ctx):
    B, S, _ = u_lat.shape
    Lc = u_ctx.shape[1]
    cq_l, ckv_l, kpe_l = jnp.split(u_lat @ w_in, C_SPLITS, axis=-1)
    cq_c, ckv_c, kpe_c = jnp.split(u_ctx @ w_in, C_SPLITS, axis=-1)
    qn_l, qp_l = mla_queries(cq_l, q_norm, w_qb)
    qp_l = apply_rope(qp_l, cos, sin)
    kpe_l = apply_rope(kpe_l[:, :, None, :], cos, sin)[:, :, 0]
    kn_l, v_l = mla_keys(ckv_l, kv_norm, w_kvb)
    kn_c, v_c = mla_keys(ckv_c, kv_norm, w_kvb)
    k_nope = jnp.concatenate([kn_c, kn_l], axis=1)
    k_pe = jnp.concatenate([kpe_c, kpe_l], axis=1)
    v = jnp.concatenate([v_c, v_l], axis=1)
    nb = S // C_QBLOCK

    def blocks(t):
        return jnp.moveaxis(t.reshape((B, nb, C_QBLOCK) + t.shape[2:]), 1, 0)

    o = lax.map(lambda qs: mla_attend(qs[0], qs[1], k_nope, k_pe, v), (blocks(qn_l), blocks(qp_l)))
    y_lat = jnp.moveaxis(o, 0, 1).reshape(B, S, C_OUT) @ w_out
    if not with_ctx:
        return y_lat, None
    qn_c, qp_c = mla_queries(cq_c, q_norm, w_qb)
    y_ctx = mla_attend(qn_c, qp_c, kn_c, kpe_c, v_c).reshape(B, Lc, C_OUT) @ w_out
    return y_lat, y_ctx


def setup_inputs(seed: int = 0) -> dict:
    key = jax.random.key(seed)
    ks = jax.random.split(key, 24)
    n_even = (DEPTH + 1) // 2
    n_odd = DEPTH // 2
    f32 = jnp.float32

    def nrm(k, shape, fan_in):
        return jax.random.normal(k, shape, f32) * fan_in ** -0.5

    dt = jnp.exp(jax.random.uniform(ks[13], (n_even, 2, B_HEADS), f32, np.log(1e-3), np.log(1e-1)))
    return {
        "x": jax.random.normal(ks[0], (BATCH, SEQ, D_MODEL), f32),
        "c": jax.random.normal(ks[1], (BATCH, D_MODEL), f32),
        "ctx": jax.random.normal(ks[2], (BATCH, CTX_LEN, D_MODEL), f32),
        "c_ctx": jax.random.normal(ks[3], (D_MODEL,), f32),
        "mod_w": nrm(ks[4], (DEPTH, D_MODEL, 6 * D_MODEL), D_MODEL),
        "mod_b": 0.02 * jax.random.normal(ks[5], (DEPTH, 6 * D_MODEL), f32),
        "norm_gains": 1.0 + 0.05 * jax.random.normal(ks[6], (DEPTH, 4, D_MODEL), f32),
        "ffn_w_gate": nrm(ks[7], (DEPTH, D_MODEL, FFN_HIDDEN), D_MODEL),
        "ffn_w_up": nrm(ks[8], (DEPTH, D_MODEL, FFN_HIDDEN), D_MODEL),
        "ffn_w_down": nrm(ks[9], (DEPTH, FFN_HIDDEN, D_MODEL), FFN_HIDDEN),
        "ab_w_in": nrm(ks[10], (n_even, D_MODEL, AB_IN), D_MODEL),
        "ab_sink": 0.5 * jax.random.normal(ks[11], (n_even, A_HEADS), f32),
        "ab_conv": nrm(ks[12], (n_even, B_CONV, 3 * B_W), B_CONV),
        "ab_a_log": jnp.log(jax.random.uniform(ks[14], (n_even, 2, B_HEADS), f32, 1.0, 16.0)),
        "ab_dt_bias": dt + jnp.log(-jnp.expm1(-dt)),
        "ab_out_norm": 1.0 + 0.05 * jax.random.normal(ks[15], (n_even, B_HEAD_DIM), f32),
        "ab_w_out": nrm(ks[16], (n_even, AB_OUT, D_MODEL), AB_OUT),
        "mla_w_in": nrm(ks[17], (n_odd, D_MODEL, C_IN), D_MODEL),
        "mla_q_norm": 1.0 + 0.05 * jax.random.normal(ks[18], (n_odd, C_Q_RANK), f32),
        "mla_kv_norm": 1.0 + 0.05 * jax.random.normal(ks[19], (n_odd, C_KV_RANK), f32),
        "mla_w_qb": nrm(ks[20], (n_odd, C_Q_RANK, C_HEADS * (C_NOPE + C_ROPE)), C_Q_RANK),
        "mla_w_kvb": nrm(ks[21], (n_odd, C_KV_RANK, C_HEADS * (C_NOPE + C_V)), C_KV_RANK),
        "mla_w_out": nrm(ks[22], (n_odd, C_OUT, D_MODEL), C_OUT),
    }


def reference(x, c, ctx, c_ctx, mod_w, mod_b, norm_gains, ffn_w_gate, ffn_w_up, ffn_w_down,
              ab_w_in, ab_sink, ab_conv, ab_a_log, ab_dt_bias, ab_out_norm, ab_w_out,
              mla_w_in, mla_q_norm, mla_kv_norm, mla_w_qb, mla_w_kvb, mla_w_out):
    S = x.shape[1]
    ROWS = S // GRID_W
    cos_a, sin_a = axial_rope_tables(ROWS, A_HEAD_DIM)
    cos_c, sin_c = axial_rope_tables(ROWS, C_ROPE)
    h, hc = x, ctx
    for layer in range(DEPTH):
        with_ctx = layer < DEPTH - 1
        gains = norm_gains[layer]
        m = jnp.split(adaln(c, mod_w[layer], mod_b[layer])[:, None, :], 6, axis=-1)
        mc = jnp.split(adaln(c_ctx, mod_w[layer], mod_b[layer])[None, None, :], 6, axis=-1)
        u = modulate(h, gains[0], m[0], m[1])
        uc = modulate(hc, gains[0], mc[0], mc[1])
        i = layer // 2
        if layer % 2 == 0:
            y, yc = hybrid_ab_mixer(u, uc, ab_w_in[i], ab_sink[i], ab_conv[i], ab_a_log[i], ab_dt_bias[i],
                                    ab_out_norm[i], ab_w_out[i], cos_a, sin_a, with_ctx)
        else:
            y, yc = mla_mixer(u, uc, mla_w_in[i], mla_q_norm[i], mla_kv_norm[i], mla_w_qb[i], mla_w_kvb[i],
                              mla_w_out[i], cos_c, sin_c, with_ctx)
        h = h + m[2] * rms_norm(y, gains[1])
        h = h + m[5] * rms_norm(swiglu(modulate(h, gains[2], m[3], m[4]),
                                       ffn_w_gate[layer], ffn_w_up[layer], ffn_w_down[layer]), gains[3])
        if with_ctx:
            hc = hc + mc[2] * rms_norm(yc, gains[1])
            hc = hc + mc[5] * rms_norm(swiglu(modulate(hc, gains[2], mc[3], mc[4]),
                                              ffn_w_gate[layer], ffn_w_up[layer], ffn_w_down[layer]), gains[3])
    return h
```

```python
import functools

import jax
import jax.numpy as jnp
from jax import lax
from jax.experimental import pallas as pl
from jax.experimental.pallas import tpu as pltpu

F32 = jnp.float32
BF16 = jnp.bfloat16

D_MODEL = 1024
GRID_W = 64
NORM_EPS = 1e-6
ROPE_BASE = 10000.0
NEG_INF = -1e30

A_HEADS = 8
A_KV_HEADS = 2
A_HEAD_DIM = 64
A_BLOCK = 128
A_GROUP = A_HEADS // A_KV_HEADS

B_HEADS = 4
B_HEAD_DIM = 128
B_CHUNK = 64
B_W = B_HEADS * B_HEAD_DIM
N_GATES = 4 * B_HEADS

C_HEADS = 16
C_NOPE = 64
C_ROPE = 32
C_V = 64
C_QK = C_NOPE + C_ROPE
C_Q_RANK = 384
C_KV_RANK = 256
C_IN_PAD = 768

A_Q_W = A_HEADS * A_HEAD_DIM
A_KV_W = A_KV_HEADS * A_HEAD_DIM

V7X_VMEM_BYTES = 64 * 1024 * 1024
VMEM_LIMIT = V7X_VMEM_BYTES - 8 * 1024 * 1024
LANES = 128
SUBLANES = 8

TOKEN_TILE = 256
SEQ_TILE = 128
FLASH_TQ = 512
FLASH_TK = 768


def _silu(x):
    return x * (1.0 / (1.0 + jnp.exp(-x)))


def _sigmoid(x):
    return 1.0 / (1.0 + jnp.exp(-x))


def _softplus(x):
    return jnp.maximum(x, 0.0) + jnp.log1p(jnp.exp(-jnp.abs(x)))


def _rms(x, gain):
    return x * lax.rsqrt(jnp.mean(x * x, axis=-1, keepdims=True) + NORM_EPS) * gain


def _mm(a, b):
    return jnp.dot(a.astype(BF16), b.astype(BF16), preferred_element_type=F32)


def _mm_nt(a, b):
    return lax.dot_general(a.astype(BF16), b.astype(BF16), (((1,), (1,)), ((), ())),
                           preferred_element_type=F32)


def _mm_f32(a, b):
    return jnp.dot(a, b, preferred_element_type=F32, precision=lax.Precision.HIGHEST)


def _iota(shape, dim):
    return lax.broadcasted_iota(jnp.int32, shape, dim)


def _rope_lanes(x, cos, sin, half):
    lane = _iota(x.shape, 1)
    first = (lane % (2 * half)) < half
    partner = jnp.where(first, pltpu.roll(x, LANES - half, 1), pltpu.roll(x, half, 1))
    return x * cos + partner * sin


def _const_spec(shape):
    return pl.BlockSpec(shape, lambda *_: (0,) * len(shape))


def _params(sem):
    return pltpu.CompilerParams(dimension_semantics=sem, vmem_limit_bytes=VMEM_LIMIT)


def _mod_kernel(cond_ref, w_ref, b_ref, o_ref):
    o_ref[0] = _mm_f32(_silu(cond_ref[...]), w_ref[0]) + b_ref[0]


def _modulation(cond, mod_w, mod_b):
    n_layers, d, n_out = mod_w.shape
    rows = cond.shape[0]
    tn = 1536
    return pl.pallas_call(
        _mod_kernel,
        grid=(n_layers, n_out // tn),
        in_specs=[
            pl.BlockSpec((rows, d), lambda l, j: (0, 0)),
            pl.BlockSpec((1, d, tn), lambda l, j: (l, 0, j)),
            pl.BlockSpec((1, 1, tn), lambda l, j: (l, 0, j)),
        ],
        out_specs=pl.BlockSpec((1, rows, tn), lambda l, j: (l, 0, j)),
        out_shape=jax.ShapeDtypeStruct((n_layers, rows, n_out), F32),
        compiler_params=_params(("parallel", "parallel")),
        name="adaln_modulation",
    )(cond, mod_w, mod_b.reshape(n_layers, 1, n_out))


def _modulate_tile(h, m, gain):
    return _rms(h, gain) * (1.0 + m[1:2]) + m[0:1]


def _ab_in_kernel(h_ref, m_ref, g_ref, wqk_ref, wvg_ref, wb_ref, cos_ref, sin_ref,
                  q_ref, k_ref, v_ref, qkvb_ref, gb_ref, gates_ref):
    m = m_ref[0, 0]
    u = _modulate_tile(h_ref[0], m[0:3], g_ref[0:1]).astype(BF16)
    qk = _mm(u, wqk_ref[...])
    n_q = A_Q_W // LANES
    for j in range((A_Q_W + A_KV_W) // LANES):
        sl = slice(j * LANES, (j + 1) * LANES)
        r = _rope_lanes(qk[:, sl], cos_ref[:, sl], sin_ref[:, sl], A_HEAD_DIM // 2)
        if j < n_q:
            q_ref[0, :, sl] = (r * (A_HEAD_DIM ** -0.5)).astype(BF16)
        else:
            k_ref[0, :, (j - n_q) * LANES:(j - n_q + 1) * LANES] = r.astype(BF16)
    vg = _mm(u, wvg_ref[...])
    v_ref[0] = vg[:, :A_KV_W].astype(BF16)
    gates_ref[0] = vg[:, A_KV_W:A_KV_W + N_GATES]
    pb = _mm(u, wb_ref[...])
    qkvb_ref[0] = pb[:, :3 * B_W]
    gb_ref[0] = pb[:, 3 * B_W:]


def _ab_in(hfull, modsel, gains, wqk, wvg, wb, cos, sin, n_ctx_tiles):
    bsz, t, d = hfull.shape
    tm = TOKEN_TILE
    tok = lambda w: pl.BlockSpec((1, tm, w), lambda b, i: (b, i, 0))
    tab = lambda w: pl.BlockSpec((tm, w), lambda b, i: (i, 0))
    out_w = (A_Q_W, A_KV_W, A_KV_W, 3 * B_W, B_W, N_GATES)
    out_dt = (BF16, BF16, BF16, F32, F32, F32)
    return pl.pallas_call(
        _ab_in_kernel,
        grid=(bsz, t // tm),
        in_specs=[
            tok(d),
            pl.BlockSpec((1, 1, 6, d), lambda b, i: (b, (i >= n_ctx_tiles).astype(jnp.int32), 0, 0)),
            _const_spec(gains.shape),
            _const_spec(wqk.shape), _const_spec(wvg.shape), _const_spec(wb.shape),
            tab(cos.shape[1]), tab(sin.shape[1]),
        ],
        out_specs=[tok(w) for w in out_w],
        out_shape=[jax.ShapeDtypeStruct((bsz, t, w), dt) for w, dt in zip(out_w, out_dt)],
        compiler_params=_params(("parallel", "parallel")),
        name="ab_in_proj",
    )(hfull, modsel, gains, wqk, wvg, wb, cos, sin)


def _window_attn_kernel(sink_ref, q_ref, kp_ref, ko_ref, kn_ref, kc_ref, vp_ref, vo_ref, vn_ref, vc_ref,
                        o_ref, *, n_ctx_tiles, n_tiles):
    t = pl.program_id(1)
    is_lat = t >= n_ctx_tiles
    big = jnp.int32(1 << 20)
    zero = jnp.int32(0)
    thr_prev = jnp.where(jnp.logical_and(is_lat, t > n_ctx_tiles), zero, big)
    thr_own = jnp.where(is_lat, zero, big)
    thr_next = jnp.where(jnp.logical_and(is_lat, t < n_tiles - 1), zero, big)
    row = _iota((A_BLOCK, A_BLOCK), 0)
    col = _iota((A_BLOCK, A_BLOCK), 1)
    ok_prev = (col - row) >= thr_prev
    ok_own = (row * 0) >= thr_own
    ok_next = (row - col) >= thr_next
    q = q_ref[0]
    for h in range(A_HEADS):
        hk = h // A_GROUP
        ks = slice(hk * A_HEAD_DIM, (hk + 1) * A_HEAD_DIM)
        qh = q[:, h * A_HEAD_DIM:(h + 1) * A_HEAD_DIM]
        s_p = jnp.where(ok_prev, _mm_nt(qh, kp_ref[0, :, ks]), NEG_INF)
        s_o = jnp.where(ok_own, _mm_nt(qh, ko_ref[0, :, ks]), NEG_INF)
        s_n = jnp.where(ok_next, _mm_nt(qh, kn_ref[0, :, ks]), NEG_INF)
        s_c = _mm_nt(qh, kc_ref[0, :, ks])
        sink = sink_ref[h]
        mx = jnp.maximum(jnp.maximum(jnp.max(s_p, axis=1, keepdims=True), jnp.max(s_o, axis=1, keepdims=True)),
                         jnp.maximum(jnp.max(s_n, axis=1, keepdims=True), jnp.max(s_c, axis=1, keepdims=True)))
        mx = jnp.maximum(mx, sink)
        p_p, p_o, p_n, p_c = (jnp.exp(s - mx) for s in (s_p, s_o, s_n, s_c))
        den = (jnp.sum(p_p, axis=1, keepdims=True) + jnp.sum(p_o, axis=1, keepdims=True)
               + jnp.sum(p_n, axis=1, keepdims=True) + jnp.sum(p_c, axis=1, keepdims=True)
               + jnp.exp(sink - mx))
        o = (_mm(p_p, vp_ref[0, :, ks]) + _mm(p_o, vo_ref[0, :, ks])
             + _mm(p_n, vn_ref[0, :, ks]) + _mm(p_c, vc_ref[0, :, ks]))
        o_ref[0, :, h * A_HEAD_DIM:(h + 1) * A_HEAD_DIM] = (o / den).astype(BF16)


def _window_attn(q, k, v, sink, lc):
    bsz, t, _ = q.shape
    n_tiles = t // A_BLOCK
    nct = lc // A_BLOCK
    blk = lambda f: pl.BlockSpec((1, A_BLOCK, A_KV_W), lambda b, i: (b, f(i), 0))
    prev = blk(lambda i: jnp.clip(i - 1, nct, n_tiles - 1))
    own = blk(lambda i: i)
    nxt = blk(lambda i: jnp.clip(i + 1, nct, n_tiles - 1))
    ctx = pl.BlockSpec((1, lc, A_KV_W), lambda b, i: (b, 0, 0))
    return pl.pallas_call(
        functools.partial(_window_attn_kernel, n_ctx_tiles=nct, n_tiles=n_tiles),
        grid=(bsz, n_tiles),
        in_specs=[pl.BlockSpec(memory_space=pltpu.SMEM),
                  pl.BlockSpec((1, A_BLOCK, A_Q_W), lambda b, i: (b, i, 0)),
                  prev, own, nxt, ctx, prev, own, nxt, ctx],
        out_specs=pl.BlockSpec((1, A_BLOCK, A_Q_W), lambda b, i: (b, i, 0)),
        out_shape=jax.ShapeDtypeStruct((bsz, t, A_Q_W), BF16),
        compiler_params=_params(("parallel", "parallel")),
        name="window_attn",
    )(sink, q, k, k, k, k, v, v, v, v)


def _unit_tri_inverse(a, leaf=16):
    n = a.shape[0]
    ri = _iota((n, n), 0)
    ci = _iota((n, n), 1)

    def same_block(size):
        return (ri // size) == (ci // size)

    d = jnp.where(same_block(leaf), a, 0.0)
    r = -d
    dp = d
    size = 2
    while size < leaf:
        dp = _mm(dp, dp)
        r = r + dp + _mm(r, dp)
        size *= 2
    size = leaf
    while size < n:
        e = jnp.where(jnp.logical_and(same_block(2 * size), jnp.logical_not(same_block(size))), a, 0.0)
        x = e + _mm(r, e)
        r = r - x - _mm(x, r)
        size *= 2
    return r


def _delta_prep_kernel(x_ref, xp_ref, xn_ref, g_ref, cw_ref, alog_ref, dtb_ref,
                       u_ref, w_ref, qh_ref, a_ref, kt_ref, egl_ref, *, n_ctx_tiles, n_tiles):
    t = pl.program_id(1)
    c = B_CHUNK
    x = x_ref[0]
    rows = _iota(x.shape, 0)
    has_prev = jnp.logical_and(t != 0, t != n_ctx_tiles)
    has_next = jnp.logical_and(t != n_ctx_tiles - 1, t != n_tiles - 1)
    prev_row = xp_ref[0][SUBLANES - 1:SUBLANES, :] * jnp.where(has_prev, 1.0, 0.0)
    next_row = xn_ref[0][0:1, :] * jnp.where(has_next, 1.0, 0.0)
    xm1 = jnp.where(rows == 0, prev_row, pltpu.roll(x, 1, 0))
    xp1 = jnp.where(rows == SEQ_TILE - 1, next_row, pltpu.roll(x, SEQ_TILE - 1, 0))
    y = _silu(cw_ref[0:1, :] * xm1 + cw_ref[1:2, :] * x + cw_ref[2:3, :] * xp1)

    gates = g_ref[0]
    ch = _iota(gates.shape, 1)
    ld = -jnp.exp(alog_ref[...]) * _softplus(gates + dtb_ref[...])
    beta = _sigmoid(gates)
    tr = _iota((SEQ_TILE, SEQ_TILE), 0)
    tc = _iota((SEQ_TILE, SEQ_TILE), 1)
    same_chunk = (tr // c) == (tc // c)
    cum_fwd = jnp.where(jnp.logical_and(same_chunk, tr >= tc), 1.0, 0.0)
    cum_bwd = jnp.where(jnp.logical_and(same_chunk, tr <= tc), 1.0, 0.0)
    gc = jnp.where(ch < B_HEADS, _mm_f32(cum_fwd, ld), _mm_f32(cum_bwd, ld))
    gct = jnp.transpose(jnp.concatenate([gc, jnp.zeros((SEQ_TILE, LANES - N_GATES), F32)], axis=1))

    ri = _iota((c, c), 0)
    ci = _iota((c, c), 1)
    for h in range(B_HEADS):
        hs = slice(h * B_HEAD_DIM, (h + 1) * B_HEAD_DIM)
        qa = y[:, h * B_HEAD_DIM:(h + 1) * B_HEAD_DIM]
        ka = y[:, B_W + h * B_HEAD_DIM:B_W + (h + 1) * B_HEAD_DIM]
        va = y[:, 2 * B_W + h * B_HEAD_DIM:2 * B_W + (h + 1) * B_HEAD_DIM]
        qa = qa * lax.rsqrt(jnp.sum(qa * qa, axis=-1, keepdims=True) + NORM_EPS) * (B_HEAD_DIM ** -0.5)
        ka = ka * lax.rsqrt(jnp.sum(ka * ka, axis=-1, keepdims=True) + NORM_EPS)
        for d in range(2):
            chn = d * B_HEADS + h
            ktails = []
            for cc in range(SEQ_TILE // c):
                rs = slice(cc * c, (cc + 1) * c)
                qc, kc, vc = qa[rs], ka[rs], va[rs]
                gcol = gc[rs, chn:chn + 1]
                grow = gct[chn:chn + 1, rs]
                bcol = beta[rs, 2 * B_HEADS + chn:2 * B_HEADS + chn + 1]
                incl = (ri >= ci) if d == 0 else (ri <= ci)
                strict = (ri > ci) if d == 0 else (ri < ci)
                decay = jnp.exp(jnp.where(incl, gcol - grow, NEG_INF))
                kb = kc * bcol
                a_kk = jnp.where(strict, _mm_nt(kb, kc) * decay, 0.0)
                r = _unit_tri_inverse(a_kk)
                eg = jnp.exp(gcol)
                rhs = jnp.concatenate([vc * bcol, kb * eg], axis=1)
                sol = rhs + _mm(r, rhs)
                last = (cc + 1) * c - 1 if d == 0 else cc * c
                glast = gc[last:last + 1, chn:chn + 1]
                u_ref[0, d, rs, hs] = sol[:, :B_HEAD_DIM]
                w_ref[0, d, rs, hs] = sol[:, B_HEAD_DIM:].astype(BF16)
                qh_ref[0, d, rs, hs] = (qc * eg).astype(BF16)
                a_ref[0, d, rs, h * c:(h + 1) * c] = (_mm_nt(qc, kc) * decay).astype(BF16)
                ktails.append(kc * jnp.exp(glast - gcol))
                egl_ref[0, 0, cc, chn:chn + 1, :] = jnp.broadcast_to(jnp.exp(glast), (1, LANES))
            kt_ref[0, d, hs, :] = jnp.transpose(jnp.concatenate(ktails, axis=0)).astype(BF16)


def _delta_prep(qkvb, gates, conv_w, alog, dtb, lc):
    bsz, t, _ = qkvb.shape
    n_tiles = t // SEQ_TILE
    nct = lc // SEQ_TILE
    per_tile = SEQ_TILE // SUBLANES
    n_rows8 = t // SUBLANES
    tile = lambda w: pl.BlockSpec((1, 2, SEQ_TILE, w), lambda b, i: (b, 0, i, 0))
    out_shape = [
        jax.ShapeDtypeStruct((bsz, 2, t, B_W), F32),
        jax.ShapeDtypeStruct((bsz, 2, t, B_W), BF16),
        jax.ShapeDtypeStruct((bsz, 2, t, B_W), BF16),
        jax.ShapeDtypeStruct((bsz, 2, t, B_HEADS * B_CHUNK), BF16),
        jax.ShapeDtypeStruct((bsz, 2, B_W, t), BF16),
        jax.ShapeDtypeStruct((bsz, n_tiles, SEQ_TILE // B_CHUNK, 2 * B_HEADS, LANES), F32),
    ]
    out_specs = [
        tile(B_W), tile(B_W), tile(B_W), tile(B_HEADS * B_CHUNK),
        pl.BlockSpec((1, 2, B_W, SEQ_TILE), lambda b, i: (b, 0, 0, i)),
        pl.BlockSpec((1, 1, SEQ_TILE // B_CHUNK, 2 * B_HEADS, LANES), lambda b, i: (b, i, 0, 0, 0)),
    ]
    return pl.pallas_call(
        functools.partial(_delta_prep_kernel, n_ctx_tiles=nct, n_tiles=n_tiles),
        grid=(bsz, n_tiles),
        in_specs=[
            pl.BlockSpec((1, SEQ_TILE, 3 * B_W), lambda b, i: (b, i, 0)),
            pl.BlockSpec((1, SUBLANES, 3 * B_W), lambda b, i: (b, jnp.maximum(i * per_tile - 1, 0), 0)),
            pl.BlockSpec((1, SUBLANES, 3 * B_W), lambda b, i: (b, jnp.minimum((i + 1) * per_tile, n_rows8 - 1), 0)),
            pl.BlockSpec((1, SEQ_TILE, N_GATES), lambda b, i: (b, i, 0)),
            _const_spec(conv_w.shape), _const_spec(alog.shape), _const_spec(dtb.shape),
        ],
        out_specs=out_specs,
        out_shape=out_shape,
        compiler_params=_params(("parallel", "parallel")),
        name="delta_prep",
    )(qkvb, qkvb, qkvb, gates, conv_w, alog, dtb)


def _delta_scan_kernel(uf_ref, wf_ref, qf_ref, af_ref, ktf_ref, eglf_ref,
                       ub_ref, wb_ref, qb_ref, ab_ref, ktb_ref, eglb_ref,
                       of_ref, ob_ref, s_ref):
    c = B_CHUNK

    @pl.when(pl.program_id(1) == 0)
    def _():
        s_ref[...] = jnp.zeros_like(s_ref)

    dirs = ((uf_ref, wf_ref, qf_ref, af_ref, ktf_ref, eglf_ref, of_ref),
            (ub_ref, wb_ref, qb_ref, ab_ref, ktb_ref, eglb_ref, ob_ref))
    for d, (u_ref, w_ref, q_ref, a_ref, kt_ref, egl_ref, o_ref) in enumerate(dirs):
        order = range(SEQ_TILE // c) if d == 0 else reversed(range(SEQ_TILE // c))
        for cc in order:
            rs = slice(cc * c, (cc + 1) * c)
            for h in range(B_HEADS):
                hs = slice(h * B_HEAD_DIM, (h + 1) * B_HEAD_DIM)
                chn = d * B_HEADS + h
                s = s_ref[chn]
                sb = s.astype(BF16)
                v_new = u_ref[0, 0, rs, hs] - _mm(w_ref[0, 0, rs, hs], sb)
                vb = v_new.astype(BF16)
                o_ref[0, rs, hs] = _mm(q_ref[0, 0, rs, hs], sb) + _mm(a_ref[0, 0, rs, h * c:(h + 1) * c], vb)
                s_ref[chn] = s * egl_ref[0, 0, cc, chn:chn + 1, :] + _mm(kt_ref[0, 0, hs, rs], vb)


def _delta_scan(u, w, qh, a, kt, egl, lc):
    bsz, _, t, _ = u.shape
    n_tiles = t // SEQ_TILE
    nct = lc // SEQ_TILE

    def tile_of(d):
        if d == 0:
            return lambda i: i
        return lambda i: jnp.where(i < nct, nct - 1 - i, n_tiles - 1 - (i - nct))

    def specs(d):
        f = tile_of(d)
        tile = lambda wd: pl.BlockSpec((1, 1, SEQ_TILE, wd), lambda b, i: (b, d, f(i), 0))
        return [tile(B_W), tile(B_W), tile(B_W), tile(B_HEADS * B_CHUNK),
                pl.BlockSpec((1, 1, B_W, SEQ_TILE), lambda b, i: (b, d, 0, f(i))),
                pl.BlockSpec((1, 1, SEQ_TILE // B_CHUNK, 2 * B_HEADS, LANES), lambda b, i: (b, f(i), 0, 0, 0))]

    out_spec = lambda d: pl.BlockSpec((1, SEQ_TILE, B_W), lambda b, i: (b, tile_of(d)(i), 0))
    return pl.pallas_call(
        _delta_scan_kernel,
        grid=(bsz, n_tiles),
        in_specs=specs(0) + specs(1),
        out_specs=[out_spec(0), out_spec(1)],
        out_shape=[jax.ShapeDtypeStruct((bsz, t, B_W), F32)] * 2,
        scratch_shapes=[pltpu.VMEM((2 * B_HEADS, B_HEAD_DIM, B_HEAD_DIM), F32)],
        compiler_params=_params(("parallel", "arbitrary")),
        name="delta_scan",
    )(u, w, qh, a, kt, egl, u, w, qh, a, kt, egl)


def _residual_ffn(h, y, m, gains, wg_ref, wu_ref, wd_ref):
    h1 = h + m[2:3] * _rms(y, gains[1:2])
    u2 = (_rms(h1, gains[2:3]) * (1.0 + m[4:5]) + m[3:4]).astype(BF16)
    act = _silu(_mm(u2, wg_ref[...])) * _mm(u2, wu_ref[...])
    f = _mm(act, wd_ref[...])
    return h1 + m[5:6] * _rms(f, gains[3:4])


def _ab_post_kernel(h_ref, m_ref, g_ref, oa_ref, of_ref, ob_ref, gb_ref, on_ref, wo_ref, wg_ref, wu_ref, wd_ref,
                    out_ref):
    o = of_ref[0] + ob_ref[0]
    gate = _silu(gb_ref[0])
    heads = []
    for hh in range(B_HEADS):
        hs = slice(hh * B_HEAD_DIM, (hh + 1) * B_HEAD_DIM)
        heads.append(_rms(o[:, hs], on_ref[...]) * gate[:, hs])
    obn = jnp.concatenate(heads, axis=1)
    y = _mm(oa_ref[0], wo_ref[:A_Q_W, :]) + _mm(obn, wo_ref[A_Q_W:, :])
    out_ref[0] = _residual_ffn(h_ref[0], y, m_ref[0, 0], g_ref[...], wg_ref, wu_ref, wd_ref)


def _mla_post_kernel(h_ref, m_ref, g_ref, o_ref, wo_ref, wg_ref, wu_ref, wd_ref, out_ref):
    y = _mm(o_ref[0], wo_ref[...])
    out_ref[0] = _residual_ffn(h_ref[0], y, m_ref[0, 0], g_ref[...], wg_ref, wu_ref, wd_ref)


def _post_call(kernel_fn, name, h, h_tile_off, modsel, gains, token_inputs, const_inputs, n_tokens, n_ctx_tiles):
    bsz, _, d = h.shape
    tm = TOKEN_TILE
    resident = lambda a: pl.BlockSpec(a.shape, lambda *_: (0,) * a.ndim, pipeline_mode=pl.Buffered(1))
    in_specs = [
        pl.BlockSpec((1, tm, d), lambda b, i: (b, i + h_tile_off, 0)),
        pl.BlockSpec((1, 1, 6, d), lambda b, i: (b, ((i + h_tile_off) >= n_ctx_tiles).astype(jnp.int32), 0, 0)),
        _const_spec(gains.shape),
    ]
    in_specs += [pl.BlockSpec((1, tm, a.shape[-1]), lambda b, i: (b, i, 0)) for a in token_inputs]
    in_specs += [resident(a) if a.size * a.dtype.itemsize > (1 << 20) else _const_spec(a.shape) for a in const_inputs]
    return pl.pallas_call(
        kernel_fn,
        grid=(bsz, n_tokens // tm),
        in_specs=in_specs,
        out_specs=pl.BlockSpec((1, tm, d), lambda b, i: (b, i, 0)),
        out_shape=jax.ShapeDtypeStruct((bsz, n_tokens, d), F32),
        compiler_params=_params(("parallel", "parallel")),
        name=name,
    )(h, modsel, gains, *token_inputs, *const_inputs)


def _mla_in_kernel(h_ref, m_ref, g_ref, win_ref, qn_ref, kvn_ref, wqb_ref, wkvb_ref,
                   cq_ref, sq_ref, ck_ref, sk_ref, q_ref, k_ref, v_ref):
    m = m_ref[0, 0]
    u = _modulate_tile(h_ref[0], m[0:3], g_ref[0:1])
    p = _mm(u, win_ref[...])
    cq = _rms(p[:, :C_Q_RANK], qn_ref[...])
    ckv = _rms(p[:, C_Q_RANK:C_Q_RANK + C_KV_RANK], kvn_ref[...])
    kpe_tile = p[:, C_Q_RANK + C_KV_RANK:]
    kpe = _rope_lanes(kpe_tile, ck_ref[...], sk_ref[...], C_ROPE // 2)[:, :C_ROPE].astype(BF16)
    q = _mm(cq, wqb_ref[...])
    kv = _mm(ckv, wkvb_ref[...])
    n_nope = C_HEADS * C_NOPE
    scale = C_QK ** -0.5
    qpe = jnp.concatenate(
        [_rope_lanes(q[:, n_nope + j * LANES:n_nope + (j + 1) * LANES],
                     cq_ref[:, j * LANES:(j + 1) * LANES], sq_ref[:, j * LANES:(j + 1) * LANES], C_ROPE // 2)
         for j in range(C_HEADS * C_ROPE // LANES)], axis=1)
    for hh in range(C_HEADS):
        qn = q[:, hh * C_NOPE:(hh + 1) * C_NOPE]
        q_ref[0, hh] = (jnp.concatenate([qn, qpe[:, hh * C_ROPE:(hh + 1) * C_ROPE]], axis=1) * scale).astype(BF16)
        k_ref[0, hh] = jnp.concatenate([kv[:, hh * C_NOPE:(hh + 1) * C_NOPE].astype(BF16), kpe], axis=1)
        v_ref[0, hh] = kv[:, n_nope + hh * C_V:n_nope + (hh + 1) * C_V].astype(BF16)


def _mla_in(hfull, modsel, gains, win, qn, kvn, wqb, wkvb, cos_q, sin_q, cos_k, sin_k, lc):
    bsz, t, d = hfull.shape
    tm = TOKEN_TILE
    nct = lc // tm
    tab = lambda a: pl.BlockSpec((tm, a.shape[1]), lambda b, i: (i, 0))
    head_tile = lambda w: pl.BlockSpec((1, C_HEADS, tm, w), lambda b, i: (b, 0, i, 0))
    return pl.pallas_call(
        _mla_in_kernel,
        grid=(bsz, t // tm),
        in_specs=[
            pl.BlockSpec((1, tm, d), lambda b, i: (b, i, 0)),
            pl.BlockSpec((1, 1, 6, d), lambda b, i: (b, (i >= nct).astype(jnp.int32), 0, 0)),
            _const_spec(gains.shape), _const_spec(win.shape), _const_spec(qn.shape), _const_spec(kvn.shape),
            _const_spec(wqb.shape), _const_spec(wkvb.shape),
            tab(cos_q), tab(sin_q), tab(cos_k), tab(sin_k),
        ],
        out_specs=[
            pl.BlockSpec((1, C_HEADS, tm, C_QK), lambda b, i: (b, 0, jnp.maximum(i - nct, 0), 0)),
            head_tile(C_QK), head_tile(C_V),
        ],
        out_shape=[
            jax.ShapeDtypeStruct((bsz, C_HEADS, t - lc, C_QK), BF16),
            jax.ShapeDtypeStruct((bsz, C_HEADS, t, C_QK), BF16),
            jax.ShapeDtypeStruct((bsz, C_HEADS, t, C_V), BF16),
        ],
        compiler_params=_params(("parallel", "arbitrary")),
        name="mla_in_proj",
    )(hfull, modsel, gains, win, qn, kvn, wqb, wkvb, cos_q, sin_q, cos_k, sin_k)


def _flash_kernel(q_ref, k_ref, v_ref, o_ref, *, n_kv_tiles, tk):
    heads = q_ref.shape[1]
    tq = q_ref.shape[2]
    for hh in range(heads):
        q = q_ref[0, hh]

        def body(j, carry, hh=hh, q=q):
            m, l, acc = carry
            start = pl.multiple_of(j * tk, tk)
            s = _mm_nt(q, k_ref[0, hh, pl.ds(start, tk), :])
            m_new = jnp.maximum(m, jnp.max(s, axis=1, keepdims=True))
            alpha = jnp.exp(m - m_new)
            p = jnp.exp(s - m_new)
            l = alpha * l + jnp.sum(p, axis=1, keepdims=True)
            acc = alpha * acc + _mm(p, v_ref[0, hh, pl.ds(start, tk), :])
            return m_new, l, acc

        init = (jnp.full((tq, 1), NEG_INF, F32), jnp.zeros((tq, 1), F32), jnp.zeros((tq, C_V), F32))
        _, l, acc = lax.fori_loop(0, n_kv_tiles, body, init)
        o_ref[0, :, hh * C_V:(hh + 1) * C_V] = (acc / l).astype(BF16)


def _flash(q, k, v):
    bsz, n_heads, s_len, _ = q.shape
    t = k.shape[2]
    tq = min(FLASH_TQ, s_len)
    tk = FLASH_TK if t % FLASH_TK == 0 else LANES
    hp = LANES // C_V
    return pl.pallas_call(
        functools.partial(_flash_kernel, n_kv_tiles=t // tk, tk=tk),
        grid=(bsz, n_heads // hp, s_len // tq),
        in_specs=[
            pl.BlockSpec((1, hp, tq, C_QK), lambda b, g, i: (b, g, i, 0)),
            pl.BlockSpec((1, hp, t, C_QK), lambda b, g, i: (b, g, 0, 0)),
            pl.BlockSpec((1, hp, t, C_V), lambda b, g, i: (b, g, 0, 0)),
        ],
        out_specs=pl.BlockSpec((1, tq, hp * C_V), lambda b, g, i: (b, i, g)),
        out_shape=jax.ShapeDtypeStruct((bsz, s_len, n_heads * C_V), BF16),
        compiler_params=_params(("parallel", "parallel", "arbitrary")),
        name="mla_flash",
    )(q, k, v)


def _axial_angles(rows, rot_dim):
    n_freq = rot_dim // 4
    inv_freq = ROPE_BASE ** (-jnp.arange(n_freq, dtype=F32) / n_freq)
    row = jnp.repeat(jnp.arange(rows, dtype=F32), GRID_W)
    col = jnp.tile(jnp.arange(GRID_W, dtype=F32), rows)
    return jnp.concatenate([row[:, None] * inv_freq, col[:, None] * inv_freq], axis=-1)


def _rope_tables(rows, rot_dim, lc, n_groups):
    ang = _axial_angles(rows, rot_dim)
    cos = jnp.concatenate([jnp.cos(ang), jnp.cos(ang)], axis=-1)
    sin = jnp.concatenate([-jnp.sin(ang), jnp.sin(ang)], axis=-1)
    cos = jnp.concatenate([jnp.ones((lc, rot_dim), F32), cos], axis=0)
    sin = jnp.concatenate([jnp.zeros((lc, rot_dim), F32), sin], axis=0)
    return jnp.tile(cos, (1, n_groups)), jnp.tile(sin, (1, n_groups))


def kernel(x, c, ctx, c_ctx, mod_w, mod_b, norm_gains, ffn_w_gate, ffn_w_up, ffn_w_down, ab_w_in, ab_sink, ab_conv, ab_a_log, ab_dt_bias, ab_out_norm, ab_w_out, mla_w_in, mla_q_norm, mla_kv_norm, mla_w_qb, mla_w_kvb, mla_w_out):
    bsz, s_len, d = x.shape
    lc = ctx.shape[1]
    rows = s_len // GRID_W
    assert d == D_MODEL and mod_w.shape[0] == 2
    assert lc % TOKEN_TILE == 0 and s_len % FLASH_TQ == 0 and s_len % TOKEN_TILE == 0
    n_ctx_tiles = lc // TOKEN_TILE

    hfull = jnp.concatenate([ctx, x], axis=1)

    n_cond = -(-(bsz + 1) // SUBLANES) * SUBLANES
    cond = jnp.concatenate([c, c_ctx[None], jnp.zeros((n_cond - bsz - 1, d), F32)], axis=0)
    mods = _modulation(cond, mod_w, mod_b)

    def mod_select(layer):
        lat = mods[layer, :bsz].reshape(bsz, 1, 6, d)
        cx = jnp.broadcast_to(mods[layer, bsz].reshape(1, 1, 6, d), (bsz, 1, 6, d))
        return jnp.concatenate([cx, lat], axis=1)

    wg = [w.astype(BF16) for w in ffn_w_gate]
    wu = [w.astype(BF16) for w in ffn_w_up]
    wd = [w.astype(BF16) for w in ffn_w_down]

    ms0 = mod_select(0)
    w_in = ab_w_in[0]
    o_q, o_k, o_v, o_b, o_g, o_dec = 0, A_Q_W, A_Q_W + A_KV_W, A_Q_W + 2 * A_KV_W, A_Q_W + 2 * A_KV_W + 3 * B_W, A_Q_W + 2 * A_KV_W + 4 * B_W
    wqk = w_in[:, o_q:o_v].astype(BF16)
    wvg = jnp.concatenate([w_in[:, o_v:o_b], w_in[:, o_dec:], jnp.zeros((d, LANES - N_GATES), F32)], axis=1).astype(BF16)
    wb = w_in[:, o_b:o_dec].astype(BF16)
    cos_a, sin_a = _rope_tables(rows, A_HEAD_DIM, lc, A_HEADS + A_KV_HEADS)
    qa, ka, va, qkvb, gb, gates = _ab_in(hfull, ms0, norm_gains[0], wqk, wvg, wb, cos_a, sin_a, n_ctx_tiles)

    oa = _window_attn(qa, ka, va, ab_sink[0], lc)

    zeros8 = jnp.zeros((2 * B_HEADS,), F32)
    alog = jnp.concatenate([ab_a_log[0].reshape(-1), zeros8]).reshape(1, N_GATES)
    dtb = jnp.concatenate([ab_dt_bias[0].reshape(-1), zeros8]).reshape(1, N_GATES)
    u, w, qh, a, kt, egl = _delta_prep(qkvb, gates, ab_conv[0], alog, dtb, lc)
    o_f, o_bw = _delta_scan(u, w, qh, a, kt, egl, lc)

    h1 = _post_call(_ab_post_kernel, "ab_post_ffn", hfull, 0, ms0, norm_gains[0],
                    [oa, o_f, o_bw, gb],
                    [ab_out_norm[0].reshape(1, B_HEAD_DIM), ab_w_out[0].astype(BF16), wg[0], wu[0], wd[0]],
                    lc + s_len, n_ctx_tiles)

    ms1 = mod_select(1)
    win = jnp.concatenate([mla_w_in[0], jnp.zeros((d, C_IN_PAD - mla_w_in.shape[2]), F32)], axis=1).astype(BF16)
    wqb = mla_w_qb[0].reshape(C_Q_RANK, C_HEADS, C_QK)
    wqb = jnp.concatenate([wqb[:, :, :C_NOPE].reshape(C_Q_RANK, -1), wqb[:, :, C_NOPE:].reshape(C_Q_RANK, -1)], axis=1).astype(BF16)
    wkvb = mla_w_kvb[0].reshape(C_KV_RANK, C_HEADS, C_NOPE + C_V)
    wkvb = jnp.concatenate([wkvb[:, :, :C_NOPE].reshape(C_KV_RANK, -1), wkvb[:, :, C_NOPE:].reshape(C_KV_RANK, -1)], axis=1).astype(BF16)
    cos_q, sin_q = _rope_tables(rows, C_ROPE, lc, C_HEADS)
    cos_k, sin_k = _rope_tables(rows, C_ROPE, lc, 1)
    pad = LANES - C_ROPE
    cos_k = jnp.concatenate([cos_k, jnp.ones((lc + s_len, pad), F32)], axis=1)
    sin_k = jnp.concatenate([sin_k, jnp.zeros((lc + s_len, pad), F32)], axis=1)
    q, k, v = _mla_in(h1, ms1, norm_gains[1], win, mla_q_norm[0].reshape(1, -1), mla_kv_norm[0].reshape(1, -1),
                      wqb, wkvb, cos_q, sin_q, cos_k, sin_k, lc)
    o = _flash(q, k, v)

    return _post_call(_mla_post_kernel, "mla_post_ffn", h1, n_ctx_tiles, ms1, norm_gains[1],
                      [o], [mla_w_out[0].astype(BF16), wg[1], wu[1], wd[1]], s_len, n_ctx_tiles)
```

```python
import functools

import jax
import jax.numpy as jnp
from jax import lax
from jax.experimental import pallas as pl
from jax.experimental.pallas import tpu as pltpu

F32 = jnp.float32
BF16 = jnp.bfloat16

D_MODEL = 1024
GRID_W = 64
NORM_EPS = 1e-6
ROPE_BASE = 10000.0
NEG_INF = -1e30

A_HEADS = 8
A_KV_HEADS = 2
A_HEAD_DIM = 64
A_BLOCK = 128
A_GROUP = A_HEADS // A_KV_HEADS

B_HEADS = 4
B_HEAD_DIM = 128
B_CHUNK = 64
B_W = B_HEADS * B_HEAD_DIM
N_GATES = 4 * B_HEADS

C_HEADS = 16
C_NOPE = 64
C_ROPE = 32
C_V = 64
C_QK = C_NOPE + C_ROPE
C_Q_RANK = 384
C_KV_RANK = 256
C_IN_PAD = 768

A_Q_W = A_HEADS * A_HEAD_DIM
A_KV_W = A_KV_HEADS * A_HEAD_DIM

V7X_VMEM_BYTES = 64 * 1024 * 1024
VMEM_LIMIT = V7X_VMEM_BYTES - 8 * 1024 * 1024
LANES = 128
SUBLANES = 8

TOKEN_TILE = 256
SEQ_TILE = 128
FLASH_TQ = 512
FLASH_TK = 768
FLASH_HEADS = 4
LOG2_E = 1.4426950408889634


def _silu(x):
    return x * (1.0 / (1.0 + jnp.exp(-x)))


def _sigmoid(x):
    return 1.0 / (1.0 + jnp.exp(-x))


def _softplus(x):
    return jnp.maximum(x, 0.0) + jnp.log1p(jnp.exp(-jnp.abs(x)))


def _rms(x, gain):
    return x * lax.rsqrt(jnp.mean(x * x, axis=-1, keepdims=True) + NORM_EPS) * gain


def _mm(a, b):
    return jnp.dot(a.astype(BF16), b.astype(BF16), preferred_element_type=F32)


def _mm_nt(a, b):
    return lax.dot_general(a.astype(BF16), b.astype(BF16), (((1,), (1,)), ((), ())),
                           preferred_element_type=F32)


def _mm_f32(a, b):
    return jnp.dot(a, b, preferred_element_type=F32, precision=lax.Precision.HIGHEST)


def _iota(shape, dim):
    return lax.broadcasted_iota(jnp.int32, shape, dim)


def _rope_lanes(x, cos, sin, half):
    lane = _iota(x.shape, 1)
    first = (lane % (2 * half)) < half
    partner = jnp.where(first, pltpu.roll(x, LANES - half, 1), pltpu.roll(x, half, 1))
    return x * cos + partner * sin


def _const_spec(shape):
    return pl.BlockSpec(shape, lambda *_: (0,) * len(shape))


def _params(sem):
    return pltpu.CompilerParams(dimension_semantics=sem, vmem_limit_bytes=VMEM_LIMIT)


def _mod_kernel(cond_ref, w_ref, b_ref, o_ref):
    o_ref[0] = _mm_f32(_silu(cond_ref[...]), w_ref[0]) + b_ref[0]


def _modulation(cond, mod_w, mod_b):
    n_layers, d, n_out = mod_w.shape
    rows = cond.shape[0]
    tn = 1536
    return pl.pallas_call(
        _mod_kernel,
        grid=(n_layers, n_out // tn),
        in_specs=[
            pl.BlockSpec((rows, d), lambda l, j: (0, 0)),
            pl.BlockSpec((1, d, tn), lambda l, j: (l, 0, j)),
            pl.BlockSpec((1, 1, tn), lambda l, j: (l, 0, j)),
        ],
        out_specs=pl.BlockSpec((1, rows, tn), lambda l, j: (l, 0, j)),
        out_shape=jax.ShapeDtypeStruct((n_layers, rows, n_out), F32),
        compiler_params=_params(("parallel", "parallel")),
        name="adaln_modulation",
    )(cond, mod_w, mod_b.reshape(n_layers, 1, n_out))


def _modulate_tile(h, m, gain):
    return _rms(h, gain) * (1.0 + m[1:2]) + m[0:1]


def _ab_in_kernel(h_ref, m_ref, g_ref, wqk_ref, wvg_ref, wb_ref, cos_ref, sin_ref,
                  q_ref, k_ref, v_ref, qkvb_ref, gb_ref, gates_ref):
    m = m_ref[0, 0]
    u = _modulate_tile(h_ref[0], m[0:3], g_ref[0:1]).astype(BF16)
    qk = _mm(u, wqk_ref[...])
    per_tile = LANES // A_HEAD_DIM
    q_scale = LOG2_E * A_HEAD_DIM ** -0.5
    for j in range((A_Q_W + A_KV_W) // LANES):
        sl = slice(j * LANES, (j + 1) * LANES)
        r = _rope_lanes(qk[:, sl], cos_ref[:, sl], sin_ref[:, sl], A_HEAD_DIM // 2)
        for i in range(per_tile):
            head = j * per_tile + i
            piece = r[:, i * A_HEAD_DIM:(i + 1) * A_HEAD_DIM]
            if head < A_HEADS:
                q_ref[0, head] = (piece * q_scale).astype(BF16)
            else:
                k_ref[0, head - A_HEADS] = piece.astype(BF16)
    vg = _mm(u, wvg_ref[...])
    ones = jnp.ones((vg.shape[0], A_HEAD_DIM), BF16)
    for hk in range(A_KV_HEADS):
        v_ref[0, hk] = jnp.concatenate([vg[:, hk * A_HEAD_DIM:(hk + 1) * A_HEAD_DIM].astype(BF16), ones], axis=1)
    gates_ref[0] = vg[:, A_KV_W:A_KV_W + N_GATES]
    pb = _mm(u, wb_ref[...])
    qkvb_ref[0] = pb[:, :3 * B_W]
    gb_ref[0] = pb[:, 3 * B_W:]


def _ab_in(hfull, modsel, gains, wqk, wvg, wb, cos, sin, n_ctx_tiles):
    bsz, t, d = hfull.shape
    tm = TOKEN_TILE
    tok = lambda w: pl.BlockSpec((1, tm, w), lambda b, i: (b, i, 0))
    tab = lambda w: pl.BlockSpec((tm, w), lambda b, i: (i, 0))
    heads = lambda n, w: pl.BlockSpec((1, n, tm, w), lambda b, i: (b, 0, i, 0))
    head_shapes = ((A_HEADS, A_HEAD_DIM), (A_KV_HEADS, A_HEAD_DIM), (A_KV_HEADS, 2 * A_HEAD_DIM))
    out_w = (3 * B_W, B_W, N_GATES)
    return pl.pallas_call(
        _ab_in_kernel,
        grid=(bsz, t // tm),
        in_specs=[
            tok(d),
            pl.BlockSpec((1, 1, 6, d), lambda b, i: (b, (i >= n_ctx_tiles).astype(jnp.int32), 0, 0)),
            _const_spec(gains.shape),
            _const_spec(wqk.shape), _const_spec(wvg.shape), _const_spec(wb.shape),
            tab(cos.shape[1]), tab(sin.shape[1]),
        ],
        out_specs=[heads(n, w) for n, w in head_shapes] + [tok(w) for w in out_w],
        out_shape=([jax.ShapeDtypeStruct((bsz, n, t, w), BF16) for n, w in head_shapes]
                   + [jax.ShapeDtypeStruct((bsz, t, w), F32) for w in out_w]),
        compiler_params=_params(("parallel", "parallel")),
        name="ab_in_proj",
    )(hfull, modsel, gains, wqk, wvg, wb, cos, sin)


def _window_attn_kernel(sink_ref, q_ref, kp_ref, ko_ref, kn_ref, kc_ref, vp_ref, vo_ref, vn_ref, vc_ref,
                        o_ref, *, n_ctx_tiles, n_tiles):
    t = pl.program_id(1)
    is_lat = t >= n_ctx_tiles
    big = jnp.int32(1 << 20)
    zero = jnp.int32(0)
    thr_prev = jnp.where(jnp.logical_and(is_lat, t > n_ctx_tiles), zero, big)
    thr_own = jnp.where(is_lat, zero, big)
    thr_next = jnp.where(jnp.logical_and(is_lat, t < n_tiles - 1), zero, big)
    lc = kc_ref.shape[2]
    n_keys = 3 * A_BLOCK + lc
    row = _iota((A_BLOCK, n_keys), 0)
    col = _iota((A_BLOCK, n_keys), 1)
    blk = col // A_BLOCK
    j = col % A_BLOCK
    dist = jnp.where(blk == 0, j - row, jnp.where(blk == 2, row - j, 0))
    thr = jnp.where(blk == 0, thr_prev, jnp.where(blk == 1, thr_own, jnp.where(blk == 2, thr_next, zero)))
    valid = dist >= thr
    group_of_row = _iota((A_GROUP * A_BLOCK, 1), 0) // A_BLOCK
    for hk in range(A_KV_HEADS):
        q = q_ref[0, hk * A_GROUP:(hk + 1) * A_GROUP].reshape(A_GROUP * A_BLOCK, A_HEAD_DIM)
        keys = jnp.concatenate([kp_ref[0, hk], ko_ref[0, hk], kn_ref[0, hk], kc_ref[0, hk]], axis=0)
        vals = jnp.concatenate([vp_ref[0, hk], vo_ref[0, hk], vn_ref[0, hk], vc_ref[0, hk]], axis=0)
        s = _mm_nt(q, keys)
        s = jnp.concatenate([jnp.where(valid, s[g * A_BLOCK:(g + 1) * A_BLOCK], NEG_INF)
                             for g in range(A_GROUP)], axis=0)
        sink = jnp.full((A_GROUP * A_BLOCK, 1), sink_ref[hk * A_GROUP] * LOG2_E, F32)
        for g in range(1, A_GROUP):
            sink = jnp.where(group_of_row == g, sink_ref[hk * A_GROUP + g] * LOG2_E, sink)
        mx = jnp.maximum(jnp.max(s, axis=1, keepdims=True), sink)
        acc = _mm(jnp.exp2(s - mx), vals)
        o = acc[:, :A_HEAD_DIM] / (acc[:, A_HEAD_DIM:] + jnp.exp2(sink - mx))
        for g in range(A_GROUP):
            h = hk * A_GROUP + g
            o_ref[0, :, h * A_HEAD_DIM:(h + 1) * A_HEAD_DIM] = o[g * A_BLOCK:(g + 1) * A_BLOCK].astype(BF16)


def _window_attn(q, k, v, sink, lc):
    bsz, _, t, _ = q.shape
    n_tiles = t // A_BLOCK
    nct = lc // A_BLOCK
    blk = lambda w, f: pl.BlockSpec((1, A_KV_HEADS, A_BLOCK, w), lambda b, i: (b, 0, f(i), 0))
    prev = lambda w: blk(w, lambda i: jnp.clip(i - 1, nct, n_tiles - 1))
    own = lambda w: blk(w, lambda i: i)
    nxt = lambda w: blk(w, lambda i: jnp.clip(i + 1, nct, n_tiles - 1))
    ctx = lambda w: pl.BlockSpec((1, A_KV_HEADS, lc, w), lambda b, i: (b, 0, 0, 0))
    kw, vw = A_HEAD_DIM, 2 * A_HEAD_DIM
    return pl.pallas_call(
        functools.partial(_window_attn_kernel, n_ctx_tiles=nct, n_tiles=n_tiles),
        grid=(bsz, n_tiles),
        in_specs=[pl.BlockSpec(memory_space=pltpu.SMEM),
                  pl.BlockSpec((1, A_HEADS, A_BLOCK, A_HEAD_DIM), lambda b, i: (b, 0, i, 0)),
                  prev(kw), own(kw), nxt(kw), ctx(kw), prev(vw), own(vw), nxt(vw), ctx(vw)],
        out_specs=pl.BlockSpec((1, A_BLOCK, A_Q_W), lambda b, i: (b, i, 0)),
        out_shape=jax.ShapeDtypeStruct((bsz, t, A_Q_W), BF16),
        compiler_params=_params(("parallel", "parallel")),
        name="window_attn",
    )(sink, q, k, k, k, k, v, v, v, v)


TRI_LEAF = 16


def _unit_tri_inverses(mats, chunk):
    n = mats[0].shape[0]
    ri = _iota((n, n), 0)
    ci = _iota((n, n), 1)

    def same_block(size):
        shift = size.bit_length() - 1
        return (ri >> shift) == (ci >> shift)

    leaf_mask = same_block(TRI_LEAF)
    ds = [jnp.where(leaf_mask, a, 0.0) for a in mats]
    rs = [-d for d in ds]
    dps = ds
    size = 2
    while size < TRI_LEAF:
        dps = [_mm(dp, dp) for dp in dps]
        rs = [r + dp + _mm(r, dp) for r, dp in zip(rs, dps)]
        size *= 2
    size = TRI_LEAF
    while size < chunk:
        ring = jnp.logical_and(same_block(2 * size), jnp.logical_not(same_block(size)))
        es = [jnp.where(ring, a, 0.0) for a in mats]
        xs = [e + _mm(r, e) for r, e in zip(rs, es)]
        rs = [r - x - _mm(x, r) for r, x in zip(rs, xs)]
        size *= 2
    return rs


def _delta_prep_kernel(x_ref, xp_ref, xn_ref, g_ref, cw_ref, alog_ref, dtb_ref,
                       u_ref, w_ref, qh_ref, a_ref, kt_ref, egl_ref, *, n_ctx_tiles, n_tiles):
    t = pl.program_id(1)
    c = B_CHUNK
    x = x_ref[0]
    rows = _iota(x.shape, 0)
    has_prev = jnp.logical_and(t != 0, t != n_ctx_tiles)
    has_next = jnp.logical_and(t != n_ctx_tiles - 1, t != n_tiles - 1)
    prev_row = xp_ref[0][SUBLANES - 1:SUBLANES, :] * jnp.where(has_prev, 1.0, 0.0)
    next_row = xn_ref[0][0:1, :] * jnp.where(has_next, 1.0, 0.0)
    xm1 = jnp.where(rows == 0, prev_row, pltpu.roll(x, 1, 0))
    xp1 = jnp.where(rows == SEQ_TILE - 1, next_row, pltpu.roll(x, SEQ_TILE - 1, 0))
    y = _silu(cw_ref[0:1, :] * xm1 + cw_ref[1:2, :] * x + cw_ref[2:3, :] * xp1)

    gates = g_ref[0]
    ch = _iota(gates.shape, 1)
    ld = -jnp.exp(alog_ref[...]) * _softplus(gates + dtb_ref[...])
    beta = _sigmoid(gates)
    tr = _iota((SEQ_TILE, SEQ_TILE), 0)
    tc = _iota((SEQ_TILE, SEQ_TILE), 1)
    same_chunk = (tr // c) == (tc // c)
    cum_fwd = jnp.where(jnp.logical_and(same_chunk, tr >= tc), 1.0, 0.0)
    cum_bwd = jnp.where(jnp.logical_and(same_chunk, tr <= tc), 1.0, 0.0)
    gc = jnp.where(ch < B_HEADS, _mm_f32(cum_fwd, ld), _mm_f32(cum_bwd, ld))
    gct = jnp.transpose(jnp.concatenate([gc, jnp.zeros((SEQ_TILE, LANES - N_GATES), F32)], axis=1))

    incl = (jnp.logical_and(same_chunk, tr >= tc), jnp.logical_and(same_chunk, tr <= tc))
    strict = (jnp.logical_and(same_chunk, tr > tc), jnp.logical_and(same_chunk, tr < tc))
    in_first_chunk = _iota((SEQ_TILE, 1), 0) < c
    zero_blk = jnp.zeros((SEQ_TILE, SEQ_TILE), F32)
    systems, rhss = [], []
    for h in range(B_HEADS):
        hs = slice(h * B_HEAD_DIM, (h + 1) * B_HEAD_DIM)
        qa = y[:, h * B_HEAD_DIM:(h + 1) * B_HEAD_DIM]
        ka = y[:, B_W + h * B_HEAD_DIM:B_W + (h + 1) * B_HEAD_DIM]
        va = y[:, 2 * B_W + h * B_HEAD_DIM:2 * B_W + (h + 1) * B_HEAD_DIM]
        qa = qa * lax.rsqrt(jnp.sum(qa * qa, axis=-1, keepdims=True) + NORM_EPS) * (B_HEAD_DIM ** -0.5)
        ka = ka * lax.rsqrt(jnp.sum(ka * ka, axis=-1, keepdims=True) + NORM_EPS)
        kk = _mm_nt(ka, ka)
        qk = _mm_nt(qa, ka)
        a_dir, rhs_dir = [], []
        for d in range(2):
            chn = d * B_HEADS + h
            gcol = gc[:, chn:chn + 1]
            grow = gct[chn:chn + 1, :]
            bcol = beta[:, 2 * B_HEADS + chn:2 * B_HEADS + chn + 1]
            decay = jnp.exp(jnp.where(incl[d], gcol - grow, NEG_INF))
            a_dir.append(jnp.where(strict[d], kk * decay, 0.0) * bcol)
            eg = jnp.exp(gcol)
            rhs_dir.append(jnp.concatenate([va * bcol, ka * (bcol * eg)], axis=1))
            last = (c - 1, 2 * c - 1) if d == 0 else (0, c)
            gl = [gc[r:r + 1, chn:chn + 1] for r in last]
            qh_ref[0, d, :, hs] = (qa * eg).astype(BF16)
            a_qk = (qk * decay).astype(BF16)
            for cc in range(SEQ_TILE // c):
                a_ref[0, d, cc * c:(cc + 1) * c, h * c:(h + 1) * c] = a_qk[cc * c:(cc + 1) * c, cc * c:(cc + 1) * c]
                egl_ref[0, 0, cc, chn:chn + 1, :] = jnp.broadcast_to(jnp.exp(gl[cc]), (1, LANES))
            ktail = ka * jnp.exp(jnp.where(in_first_chunk, gl[0], gl[1]) - gcol)
            kt_ref[0, d, hs, :] = jnp.transpose(ktail).astype(BF16)
        systems.append(jnp.concatenate([jnp.concatenate([a_dir[0], zero_blk], axis=1),
                                        jnp.concatenate([zero_blk, a_dir[1]], axis=1)], axis=0))
        rhss.append(jnp.concatenate(rhs_dir, axis=0))
    inverses = _unit_tri_inverses(systems, c)
    sols = [rhs + _mm(r, rhs) for r, rhs in zip(inverses, rhss)]
    for h in range(B_HEADS):
        hs = slice(h * B_HEAD_DIM, (h + 1) * B_HEAD_DIM)
        for d in range(2):
            rows_d = slice(d * SEQ_TILE, (d + 1) * SEQ_TILE)
            u_ref[0, d, :, hs] = sols[h][rows_d, :B_HEAD_DIM]
            w_ref[0, d, :, hs] = sols[h][rows_d, B_HEAD_DIM:].astype(BF16)


def _delta_prep(qkvb, gates, conv_w, alog, dtb, lc):
    bsz, t, _ = qkvb.shape
    n_tiles = t // SEQ_TILE
    nct = lc // SEQ_TILE
    per_tile = SEQ_TILE // SUBLANES
    n_rows8 = t // SUBLANES
    tile = lambda w: pl.BlockSpec((1, 2, SEQ_TILE, w), lambda b, i: (b, 0, i, 0))
    out_shape = [
        jax.ShapeDtypeStruct((bsz, 2, t, B_W), F32),
        jax.ShapeDtypeStruct((bsz, 2, t, B_W), BF16),
        jax.ShapeDtypeStruct((bsz, 2, t, B_W), BF16),
        jax.ShapeDtypeStruct((bsz, 2, t, B_HEADS * B_CHUNK), BF16),
        jax.ShapeDtypeStruct((bsz, 2, B_W, t), BF16),
        jax.ShapeDtypeStruct((bsz, n_tiles, SEQ_TILE // B_CHUNK, 2 * B_HEADS, LANES), F32),
    ]
    out_specs = [
        tile(B_W), tile(B_W), tile(B_W), tile(B_HEADS * B_CHUNK),
        pl.BlockSpec((1, 2, B_W, SEQ_TILE), lambda b, i: (b, 0, 0, i)),
        pl.BlockSpec((1, 1, SEQ_TILE // B_CHUNK, 2 * B_HEADS, LANES), lambda b, i: (b, i, 0, 0, 0)),
    ]
    return pl.pallas_call(
        functools.partial(_delta_prep_kernel, n_ctx_tiles=nct, n_tiles=n_tiles),
        grid=(bsz, n_tiles),
        in_specs=[
            pl.BlockSpec((1, SEQ_TILE, 3 * B_W), lambda b, i: (b, i, 0)),
            pl.BlockSpec((1, SUBLANES, 3 * B_W), lambda b, i: (b, jnp.maximum(i * per_tile - 1, 0), 0)),
            pl.BlockSpec((1, SUBLANES, 3 * B_W), lambda b, i: (b, jnp.minimum((i + 1) * per_tile, n_rows8 - 1), 0)),
            pl.BlockSpec((1, SEQ_TILE, N_GATES), lambda b, i: (b, i, 0)),
            _const_spec(conv_w.shape), _const_spec(alog.shape), _const_spec(dtb.shape),
        ],
        out_specs=out_specs,
        out_shape=out_shape,
        compiler_params=_params(("parallel", "parallel")),
        name="delta_prep",
    )(qkvb, qkvb, qkvb, gates, conv_w, alog, dtb)


def _delta_scan_kernel(uf_ref, wf_ref, qf_ref, af_ref, ktf_ref, eglf_ref,
                       ub_ref, wb_ref, qb_ref, ab_ref, ktb_ref, eglb_ref,
                       of_ref, ob_ref, s_ref):
    c = B_CHUNK

    @pl.when(pl.program_id(1) == 0)
    def _():
        s_ref[...] = jnp.zeros_like(s_ref)

    dirs = ((uf_ref, wf_ref, qf_ref, af_ref, ktf_ref, eglf_ref, of_ref),
            (ub_ref, wb_ref, qb_ref, ab_ref, ktb_ref, eglb_ref, ob_ref))
    for d, (u_ref, w_ref, q_ref, a_ref, kt_ref, egl_ref, o_ref) in enumerate(dirs):
        order = range(SEQ_TILE // c) if d == 0 else reversed(range(SEQ_TILE // c))
        for cc in order:
            rs = slice(cc * c, (cc + 1) * c)
            for h in range(B_HEADS):
                hs = slice(h * B_HEAD_DIM, (h + 1) * B_HEAD_DIM)
                chn = d * B_HEADS + h
                s = s_ref[chn]
                sb = s.astype(BF16)
                v_new = u_ref[0, 0, rs, hs] - _mm(w_ref[0, 0, rs, hs], sb)
                vb = v_new.astype(BF16)
                o_ref[0, rs, hs] = _mm(q_ref[0, 0, rs, hs], sb) + _mm(a_ref[0, 0, rs, h * c:(h + 1) * c], vb)
                s_ref[chn] = s * egl_ref[0, 0, cc, chn:chn + 1, :] + _mm(kt_ref[0, 0, hs, rs], vb)


def _delta_scan(u, w, qh, a, kt, egl, lc):
    bsz, _, t, _ = u.shape
    n_tiles = t // SEQ_TILE
    nct = lc // SEQ_TILE

    def tile_of(d):
        if d == 0:
            return lambda i: i
        return lambda i: jnp.where(i < nct, nct - 1 - i, n_tiles - 1 - (i - nct))

    def specs(d):
        f = tile_of(d)
        tile = lambda wd: pl.BlockSpec((1, 1, SEQ_TILE, wd), lambda b, i: (b, d, f(i), 0))
        return [tile(B_W), tile(B_W), tile(B_W), tile(B_HEADS * B_CHUNK),
                pl.BlockSpec((1, 1, B_W, SEQ_TILE), lambda b, i: (b, d, 0, f(i))),
                pl.BlockSpec((1, 1, SEQ_TILE // B_CHUNK, 2 * B_HEADS, LANES), lambda b, i: (b, f(i), 0, 0, 0))]

    out_spec = lambda d: pl.BlockSpec((1, SEQ_TILE, B_W), lambda b, i: (b, tile_of(d)(i), 0))
    return pl.pallas_call(
        _delta_scan_kernel,
        grid=(bsz, n_tiles),
        in_specs=specs(0) + specs(1),
        out_specs=[out_spec(0), out_spec(1)],
        out_shape=[jax.ShapeDtypeStruct((bsz, t, B_W), F32)] * 2,
        scratch_shapes=[pltpu.VMEM((2 * B_HEADS, B_HEAD_DIM, B_HEAD_DIM), F32)],
        compiler_params=_params(("parallel", "arbitrary")),
        name="delta_scan",
    )(u, w, qh, a, kt, egl, u, w, qh, a, kt, egl)


def _residual_ffn(h, y, m, gains, wg_ref, wu_ref, wd_ref):
    h1 = h + m[2:3] * _rms(y, gains[1:2])
    u2 = (_rms(h1, gains[2:3]) * (1.0 + m[4:5]) + m[3:4]).astype(BF16)
    act = _silu(_mm(u2, wg_ref[...])) * _mm(u2, wu_ref[...])
    f = _mm(act, wd_ref[...])
    return h1 + m[5:6] * _rms(f, gains[3:4])


def _ab_post_kernel(h_ref, m_ref, g_ref, oa_ref, of_ref, ob_ref, gb_ref, on_ref, wo_ref, wg_ref, wu_ref, wd_ref,
                    out_ref):
    o = of_ref[0] + ob_ref[0]
    gate = _silu(gb_ref[0])
    heads = []
    for hh in range(B_HEADS):
        hs = slice(hh * B_HEAD_DIM, (hh + 1) * B_HEAD_DIM)
        heads.append(_rms(o[:, hs], on_ref[...]) * gate[:, hs])
    obn = jnp.concatenate(heads, axis=1)
    y = _mm(oa_ref[0], wo_ref[:A_Q_W, :]) + _mm(obn, wo_ref[A_Q_W:, :])
    out_ref[0] = _residual_ffn(h_ref[0], y, m_ref[0, 0], g_ref[...], wg_ref, wu_ref, wd_ref)


def _mla_post_kernel(h_ref, m_ref, g_ref, o_ref, wo_ref, wg_ref, wu_ref, wd_ref, out_ref):
    y = _mm(o_ref[0], wo_ref[...])
    out_ref[0] = _residual_ffn(h_ref[0], y, m_ref[0, 0], g_ref[...], wg_ref, wu_ref, wd_ref)


def _post_call(kernel_fn, name, h, h_tile_off, modsel, gains, token_inputs, const_inputs, n_tokens, n_ctx_tiles):
    bsz, _, d = h.shape
    tm = TOKEN_TILE
    resident = lambda a: pl.BlockSpec(a.shape, lambda *_: (0,) * a.ndim, pipeline_mode=pl.Buffered(1))
    in_specs = [
        pl.BlockSpec((1, tm, d), lambda b, i: (b, i + h_tile_off, 0)),
        pl.BlockSpec((1, 1, 6, d), lambda b, i: (b, ((i + h_tile_off) >= n_ctx_tiles).astype(jnp.int32), 0, 0)),
        _const_spec(gains.shape),
    ]
    in_specs += [pl.BlockSpec((1, tm, a.shape[-1]), lambda b, i: (b, i, 0)) for a in token_inputs]
    in_specs += [resident(a) if a.size * a.dtype.itemsize > (1 << 20) else _const_spec(a.shape) for a in const_inputs]
    return pl.pallas_call(
        kernel_fn,
        grid=(bsz, n_tokens // tm),
        in_specs=in_specs,
        out_specs=pl.BlockSpec((1, tm, d), lambda b, i: (b, i, 0)),
        out_shape=jax.ShapeDtypeStruct((bsz, n_tokens, d), F32),
        compiler_params=_params(("parallel", "parallel")),
        name=name,
    )(h, modsel, gains, *token_inputs, *const_inputs)


def _mla_in_kernel(h_ref, m_ref, g_ref, win_ref, qn_ref, kvn_ref, wqt_ref, wk_ref, wvt_ref,
                   cq_ref, sq_ref, ck_ref, sk_ref, qt_ref, k_ref, vt_ref):
    m = m_ref[0, 0]
    u = _modulate_tile(h_ref[0], m[0:3], g_ref[0:1])
    p = _mm(u, win_ref[...])
    cq = _rms(p[:, :C_Q_RANK], qn_ref[...]).astype(BF16)
    ckv = _rms(p[:, C_Q_RANK:C_Q_RANK + C_KV_RANK], kvn_ref[...]).astype(BF16)
    kpe_tile = p[:, C_Q_RANK + C_KV_RANK:]
    kpe = _rope_lanes(kpe_tile, ck_ref[...], sk_ref[...], C_ROPE // 2)[:, :C_ROPE].astype(BF16)
    qt = _mm_nt(wqt_ref[...], cq)
    kn = _mm(ckv, wk_ref[...])
    vt = _mm_nt(wvt_ref[...], ckv)
    n_nope = C_HEADS * C_NOPE
    half = C_ROPE // 2
    scale = LOG2_E * C_QK ** -0.5
    cos, sin = cq_ref[...], sq_ref[...]
    ones = jnp.ones((C_V, qt.shape[1]), BF16)
    for hh in range(C_HEADS):
        x1 = qt[n_nope + hh * C_ROPE:n_nope + hh * C_ROPE + half]
        x2 = qt[n_nope + hh * C_ROPE + half:n_nope + (hh + 1) * C_ROPE]
        qh = jnp.concatenate([qt[hh * C_NOPE:(hh + 1) * C_NOPE], x1 * cos - x2 * sin, x1 * sin + x2 * cos], axis=0)
        qt_ref[0, hh] = (qh * scale).astype(BF16)
        k_ref[0, hh] = jnp.concatenate([kn[:, hh * C_NOPE:(hh + 1) * C_NOPE].astype(BF16), kpe], axis=1)
        vt_ref[0, hh] = jnp.concatenate([vt[hh * C_V:(hh + 1) * C_V].astype(BF16), ones], axis=0)


def _mla_in(hfull, modsel, gains, win, qn, kvn, wqt, wk, wvt, cos_q, sin_q, cos_k, sin_k, lc):
    bsz, t, d = hfull.shape
    tm = TOKEN_TILE
    nct = lc // tm
    tab = lambda a: pl.BlockSpec((tm, a.shape[1]), lambda b, i: (i, 0))
    tab_t = lambda a: pl.BlockSpec((a.shape[0], tm), lambda b, i: (0, i))
    return pl.pallas_call(
        _mla_in_kernel,
        grid=(bsz, t // tm),
        in_specs=[
            pl.BlockSpec((1, tm, d), lambda b, i: (b, i, 0)),
            pl.BlockSpec((1, 1, 6, d), lambda b, i: (b, (i >= nct).astype(jnp.int32), 0, 0)),
            _const_spec(gains.shape), _const_spec(win.shape), _const_spec(qn.shape), _const_spec(kvn.shape),
            _const_spec(wqt.shape), _const_spec(wk.shape), _const_spec(wvt.shape),
            tab_t(cos_q), tab_t(sin_q), tab(cos_k), tab(sin_k),
        ],
        out_specs=[
            pl.BlockSpec((1, C_HEADS, C_QK, tm), lambda b, i: (b, 0, 0, jnp.maximum(i - nct, 0))),
            pl.BlockSpec((1, C_HEADS, tm, C_QK), lambda b, i: (b, 0, i, 0)),
            pl.BlockSpec((1, C_HEADS, 2 * C_V, tm), lambda b, i: (b, 0, 0, i)),
        ],
        out_shape=[
            jax.ShapeDtypeStruct((bsz, C_HEADS, C_QK, t - lc), BF16),
            jax.ShapeDtypeStruct((bsz, C_HEADS, t, C_QK), BF16),
            jax.ShapeDtypeStruct((bsz, C_HEADS, 2 * C_V, t), BF16),
        ],
        compiler_params=_params(("parallel", "arbitrary")),
        name="mla_in_proj",
    )(hfull, modsel, gains, win, qn, kvn, wqt, wk, wvt, cos_q, sin_q, cos_k, sin_k)


def _flash_kernel(qt_ref, k_ref, vt_ref, o_ref, st_ref, *, n_kv_tiles, tk):
    heads = qt_ref.shape[1]
    tq = qt_ref.shape[3]

    def scores(j, slot):
        start = pl.multiple_of(j * tk, tk)
        for hh in range(heads):
            st_ref[slot, hh] = jnp.dot(k_ref[0, hh, pl.ds(start, tk), :], qt_ref[0, hh],
                                       preferred_element_type=F32)

    def absorb(j, slot, state):
        start = pl.multiple_of(j * tk, tk)
        out = []
        for hh in range(heads):
            m, acc = state[hh]
            m_new = jnp.maximum(m, jnp.max(st_ref[slot, hh], axis=0, keepdims=True))
            pt = jnp.exp2(st_ref[slot, hh] - m_new).astype(BF16)
            acc = jnp.exp2(m - m_new) * acc + jnp.dot(vt_ref[0, hh, :, pl.ds(start, tk)], pt,
                                                      preferred_element_type=F32)
            out.append((m_new, acc))
        return tuple(out)

    def body(i, state):
        j = 2 * i
        scores(j + 1, 1)
        state = absorb(j, 0, state)
        scores(j + 2, 0)
        return absorb(j + 1, 1, state)

    assert n_kv_tiles % 2 == 1
    state = tuple((jnp.full((1, tq), NEG_INF, F32), jnp.zeros((2 * C_V, tq), F32)) for _ in range(heads))
    scores(0, 0)
    state = lax.fori_loop(0, n_kv_tiles // 2, body, state)
    final = absorb(n_kv_tiles - 1, 0, state)
    for hh in range(heads):
        acc = jnp.transpose(final[hh][1])
        o_ref[0, :, hh * C_V:(hh + 1) * C_V] = (acc[:, :C_V] / acc[:, C_V:]).astype(BF16)


def _flash(qt, k, vt):
    bsz, n_heads, _, s_len = qt.shape
    t = k.shape[2]
    tq = min(FLASH_TQ, s_len)
    tk = FLASH_TK if t % FLASH_TK == 0 else LANES
    hp = FLASH_HEADS
    return pl.pallas_call(
        functools.partial(_flash_kernel, n_kv_tiles=t // tk, tk=tk),
        grid=(bsz, n_heads // hp, s_len // tq),
        in_specs=[
            pl.BlockSpec((1, hp, C_QK, tq), lambda b, g, i: (b, g, 0, i)),
            pl.BlockSpec((1, hp, t, C_QK), lambda b, g, i: (b, g, 0, 0)),
            pl.BlockSpec((1, hp, 2 * C_V, t), lambda b, g, i: (b, g, 0, 0)),
        ],
        out_specs=pl.BlockSpec((1, tq, hp * C_V), lambda b, g, i: (b, i, g)),
        out_shape=jax.ShapeDtypeStruct((bsz, s_len, n_heads * C_V), BF16),
        scratch_shapes=[pltpu.VMEM((2, hp, tk, tq), F32)],
        compiler_params=_params(("parallel", "parallel", "arbitrary")),
        name="mla_flash",
    )(qt, k, vt)


def _axial_angles(rows, rot_dim):
    n_freq = rot_dim // 4
    inv_freq = ROPE_BASE ** (-jnp.arange(n_freq, dtype=F32) / n_freq)
    row = jnp.repeat(jnp.arange(rows, dtype=F32), GRID_W)
    col = jnp.tile(jnp.arange(GRID_W, dtype=F32), rows)
    return jnp.concatenate([row[:, None] * inv_freq, col[:, None] * inv_freq], axis=-1)


def _rope_tables(rows, rot_dim, lc, n_groups):
    ang = _axial_angles(rows, rot_dim)
    cos = jnp.concatenate([jnp.cos(ang), jnp.cos(ang)], axis=-1)
    sin = jnp.concatenate([-jnp.sin(ang), jnp.sin(ang)], axis=-1)
    cos = jnp.concatenate([jnp.ones((lc, rot_dim), F32), cos], axis=0)
    sin = jnp.concatenate([jnp.zeros((lc, rot_dim), F32), sin], axis=0)
    return jnp.tile(cos, (1, n_groups)), jnp.tile(sin, (1, n_groups))


def kernel(x, c, ctx, c_ctx, mod_w, mod_b, norm_gains, ffn_w_gate, ffn_w_up, ffn_w_down, ab_w_in, ab_sink, ab_conv, ab_a_log, ab_dt_bias, ab_out_norm, ab_w_out, mla_w_in, mla_q_norm, mla_kv_norm, mla_w_qb, mla_w_kvb, mla_w_out):
    bsz, s_len, d = x.shape
    lc = ctx.shape[1]
    rows = s_len // GRID_W
    assert d == D_MODEL and mod_w.shape[0] == 2
    assert lc % TOKEN_TILE == 0 and s_len % FLASH_TQ == 0 and s_len % TOKEN_TILE == 0
    n_ctx_tiles = lc // TOKEN_TILE

    hfull = jnp.concatenate([ctx, x], axis=1)

    n_cond = -(-(bsz + 1) // SUBLANES) * SUBLANES
    cond = jnp.concatenate([c, c_ctx[None], jnp.zeros((n_cond - bsz - 1, d), F32)], axis=0)
    mods = _modulation(cond, mod_w, mod_b)

    def mod_select(layer):
        lat = mods[layer, :bsz].reshape(bsz, 1, 6, d)
        cx = jnp.broadcast_to(mods[layer, bsz].reshape(1, 1, 6, d), (bsz, 1, 6, d))
        return jnp.concatenate([cx, lat], axis=1)

    wg = [w.astype(BF16) for w in ffn_w_gate]
    wu = [w.astype(BF16) for w in ffn_w_up]
    wd = [w.astype(BF16) for w in ffn_w_down]

    ms0 = mod_select(0)
    w_in = ab_w_in[0]
    o_q, o_k, o_v, o_b, o_g, o_dec = 0, A_Q_W, A_Q_W + A_KV_W, A_Q_W + 2 * A_KV_W, A_Q_W + 2 * A_KV_W + 3 * B_W, A_Q_W + 2 * A_KV_W + 4 * B_W
    wqk = w_in[:, o_q:o_v].astype(BF16)
    wvg = jnp.concatenate([w_in[:, o_v:o_b], w_in[:, o_dec:], jnp.zeros((d, LANES - N_GATES), F32)], axis=1).astype(BF16)
    wb = w_in[:, o_b:o_dec].astype(BF16)
    cos_a, sin_a = _rope_tables(rows, A_HEAD_DIM, lc, A_HEADS + A_KV_HEADS)
    qa, ka, va, qkvb, gb, gates = _ab_in(hfull, ms0, norm_gains[0], wqk, wvg, wb, cos_a, sin_a, n_ctx_tiles)

    oa = _window_attn(qa, ka, va, ab_sink[0], lc)

    zeros8 = jnp.zeros((2 * B_HEADS,), F32)
    alog = jnp.concatenate([ab_a_log[0].reshape(-1), zeros8]).reshape(1, N_GATES)
    dtb = jnp.concatenate([ab_dt_bias[0].reshape(-1), zeros8]).reshape(1, N_GATES)
    u, w, qh, a, kt, egl = _delta_prep(qkvb, gates, ab_conv[0], alog, dtb, lc)
    o_f, o_bw = _delta_scan(u, w, qh, a, kt, egl, lc)

    h1 = _post_call(_ab_post_kernel, "ab_post_ffn", hfull, 0, ms0, norm_gains[0],
                    [oa, o_f, o_bw, gb],
                    [ab_out_norm[0].reshape(1, B_HEAD_DIM), ab_w_out[0].astype(BF16), wg[0], wu[0], wd[0]],
                    lc + s_len, n_ctx_tiles)

    ms1 = mod_select(1)
    win = jnp.concatenate([mla_w_in[0], jnp.zeros((d, C_IN_PAD - mla_w_in.shape[2]), F32)], axis=1).astype(BF16)
    wqb = mla_w_qb[0].reshape(C_Q_RANK, C_HEADS, C_QK)
    wqt = jnp.concatenate([wqb[:, :, :C_NOPE].reshape(C_Q_RANK, -1), wqb[:, :, C_NOPE:].reshape(C_Q_RANK, -1)], axis=1).T.astype(BF16)
    wkvb = mla_w_kvb[0].reshape(C_KV_RANK, C_HEADS, C_NOPE + C_V)
    wk = wkvb[:, :, :C_NOPE].reshape(C_KV_RANK, -1).astype(BF16)
    wvt = wkvb[:, :, C_NOPE:].reshape(C_KV_RANK, -1).T.astype(BF16)
    ang = _axial_angles(rows, C_ROPE)
    cos_q = jnp.concatenate([jnp.ones((lc, C_ROPE // 2), F32), jnp.cos(ang)], axis=0).T
    sin_q = jnp.concatenate([jnp.zeros((lc, C_ROPE // 2), F32), jnp.sin(ang)], axis=0).T
    cos_k, sin_k = _rope_tables(rows, C_ROPE, lc, 1)
    pad = LANES - C_ROPE
    cos_k = jnp.concatenate([cos_k, jnp.ones((lc + s_len, pad), F32)], axis=1)
    sin_k = jnp.concatenate([sin_k, jnp.zeros((lc + s_len, pad), F32)], axis=1)
    qt, k, vt = _mla_in(h1, ms1, norm_gains[1], win, mla_q_norm[0].reshape(1, -1), mla_kv_norm[0].reshape(1, -1),
                        wqt, wk, wvt, cos_q, sin_q, cos_k, sin_k, lc)
    o = _flash(qt, k, vt)

    return _post_call(_mla_post_kernel, "mla_post_ffn", h1, n_ctx_tiles, ms1, norm_gains[1],
                      [o], [mla_w_out[0].astype(BF16), wg[1], wu[1], wd[1]], s_len, n_ctx_tiles)
```

```python
import functools

import jax
import jax.numpy as jnp
from jax import lax
from jax.experimental import pallas as pl
from jax.experimental.pallas import tpu as pltpu

F32 = jnp.float32
BF16 = jnp.bfloat16

D_MODEL = 1024
GRID_W = 64
NORM_EPS = 1e-6
ROPE_BASE = 10000.0
NEG_INF = -1e30

A_HEADS = 8
A_KV_HEADS = 2
A_HEAD_DIM = 64
A_BLOCK = 128
A_GROUP = A_HEADS // A_KV_HEADS

B_HEADS = 4
B_HEAD_DIM = 128
B_CHUNK = 64
B_W = B_HEADS * B_HEAD_DIM
N_GATES = 4 * B_HEADS

C_HEADS = 16
C_NOPE = 64
C_ROPE = 32
C_V = 64
C_QK = C_NOPE + C_ROPE
C_Q_RANK = 384
C_KV_RANK = 256
C_IN_PAD = 768

A_Q_W = A_HEADS * A_HEAD_DIM
A_KV_W = A_KV_HEADS * A_HEAD_DIM

V7X_VMEM_BYTES = 64 * 1024 * 1024
VMEM_LIMIT = V7X_VMEM_BYTES - 8 * 1024 * 1024
LANES = 128
SUBLANES = 8

TOKEN_TILE = 256
SEQ_TILE = 128
FLASH_TQ = 512
FLASH_TK = 768
FLASH_HEADS = 4
FLASH_LOOKAHEAD = 2
FLASH_UNROLL = 2
LOG2_E = 1.4426950408889634


def _silu(x):
    return x * (1.0 / (1.0 + jnp.exp(-x)))


def _sigmoid(x):
    return 1.0 / (1.0 + jnp.exp(-x))


def _softplus(x):
    return jnp.maximum(x, 0.0) + jnp.log1p(jnp.exp(-jnp.abs(x)))


def _rms(x, gain):
    return x * lax.rsqrt(jnp.mean(x * x, axis=-1, keepdims=True) + NORM_EPS) * gain


def _mm(a, b):
    return jnp.dot(a.astype(BF16), b.astype(BF16), preferred_element_type=F32)


def _mm_nt(a, b):
    return lax.dot_general(a.astype(BF16), b.astype(BF16), (((1,), (1,)), ((), ())),
                           preferred_element_type=F32)


def _mm_f32(a, b):
    return jnp.dot(a, b, preferred_element_type=F32, precision=lax.Precision.HIGHEST)


def _iota(shape, dim):
    return lax.broadcasted_iota(jnp.int32, shape, dim)


def _rope_lanes(x, cos, sin, half):
    lane = _iota(x.shape, 1)
    first = (lane % (2 * half)) < half
    partner = jnp.where(first, pltpu.roll(x, LANES - half, 1), pltpu.roll(x, half, 1))
    return x * cos + partner * sin


def _const_spec(shape):
    return pl.BlockSpec(shape, lambda *_: (0,) * len(shape))


def _params(sem):
    return pltpu.CompilerParams(dimension_semantics=sem, vmem_limit_bytes=VMEM_LIMIT)


def _mod_kernel(cond_ref, w_ref, b_ref, o_ref):
    o_ref[0] = _mm_f32(_silu(cond_ref[...]), w_ref[0]) + b_ref[0]


def _modulation(cond, mod_w, mod_b):
    n_layers, d, n_out = mod_w.shape
    rows = cond.shape[0]
    tn = 1536
    return pl.pallas_call(
        _mod_kernel,
        grid=(n_layers, n_out // tn),
        in_specs=[
            pl.BlockSpec((rows, d), lambda l, j: (0, 0)),
            pl.BlockSpec((1, d, tn), lambda l, j: (l, 0, j)),
            pl.BlockSpec((1, 1, tn), lambda l, j: (l, 0, j)),
        ],
        out_specs=pl.BlockSpec((1, rows, tn), lambda l, j: (l, 0, j)),
        out_shape=jax.ShapeDtypeStruct((n_layers, rows, n_out), F32),
        compiler_params=_params(("parallel", "parallel")),
        name="adaln_modulation",
    )(cond, mod_w, mod_b.reshape(n_layers, 1, n_out))


def _modulate_tile(h, m, gain):
    return _rms(h, gain) * (1.0 + m[1:2]) + m[0:1]


def _ab_in_kernel(hc_ref, hx_ref, m_ref, g_ref, wqk_ref, wvg_ref, wb_ref, cos_ref, sin_ref,
                  q_ref, k_ref, v_ref, qkvb_ref, gb_ref, gates_ref, *, n_ctx_tiles):
    m = m_ref[0, 0]
    u = _modulate_tile(_stream_tile(hc_ref, hx_ref, n_ctx_tiles), m[0:3], g_ref[0:1]).astype(BF16)
    qk = _mm(u, wqk_ref[...])
    per_tile = LANES // A_HEAD_DIM
    q_scale = LOG2_E * A_HEAD_DIM ** -0.5
    for j in range((A_Q_W + A_KV_W) // LANES):
        sl = slice(j * LANES, (j + 1) * LANES)
        r = _rope_lanes(qk[:, sl], cos_ref[:, sl], sin_ref[:, sl], A_HEAD_DIM // 2)
        for i in range(per_tile):
            head = j * per_tile + i
            piece = r[:, i * A_HEAD_DIM:(i + 1) * A_HEAD_DIM]
            if head < A_HEADS:
                q_ref[0, head] = (piece * q_scale).astype(BF16)
            else:
                k_ref[0, head - A_HEADS] = piece.astype(BF16)
    vg = _mm(u, wvg_ref[...])
    ones = jnp.ones((vg.shape[0], A_HEAD_DIM), BF16)
    for hk in range(A_KV_HEADS):
        v_ref[0, hk] = jnp.concatenate([vg[:, hk * A_HEAD_DIM:(hk + 1) * A_HEAD_DIM].astype(BF16), ones], axis=1)
    gates_ref[0] = vg[:, A_KV_W:A_KV_W + N_GATES]
    pb = _mm(u, wb_ref[...])
    qkvb_ref[0] = pb[:, :3 * B_W]
    gb_ref[0] = pb[:, 3 * B_W:]


def _ab_in(ctx, x, modsel, gains, wqk, wvg, wb, cos, sin, n_ctx_tiles):
    bsz, s_len, d = x.shape
    t = ctx.shape[1] + s_len
    tm = TOKEN_TILE
    tok = lambda w: pl.BlockSpec((1, tm, w), lambda b, i: (b, i, 0))
    tab = lambda w: pl.BlockSpec((tm, w), lambda b, i: (i, 0))
    heads = lambda n, w: pl.BlockSpec((1, n, tm, w), lambda b, i: (b, 0, i, 0))
    head_shapes = ((A_HEADS, A_HEAD_DIM), (A_KV_HEADS, A_HEAD_DIM), (A_KV_HEADS, 2 * A_HEAD_DIM))
    out_w = (3 * B_W, B_W, N_GATES)
    return pl.pallas_call(
        functools.partial(_ab_in_kernel, n_ctx_tiles=n_ctx_tiles),
        grid=(bsz, t // tm),
        in_specs=_stream_specs(d, n_ctx_tiles) + [
            pl.BlockSpec((1, 1, 6, d), lambda b, i: (b, (i >= n_ctx_tiles).astype(jnp.int32), 0, 0)),
            _const_spec(gains.shape),
            _const_spec(wqk.shape), _const_spec(wvg.shape), _const_spec(wb.shape),
            tab(cos.shape[1]), tab(sin.shape[1]),
        ],
        out_specs=[heads(n, w) for n, w in head_shapes] + [tok(w) for w in out_w],
        out_shape=([jax.ShapeDtypeStruct((bsz, n, t, w), BF16) for n, w in head_shapes]
                   + [jax.ShapeDtypeStruct((bsz, t, w), F32) for w in out_w]),
        compiler_params=_params(("parallel", "parallel")),
        name="ab_in_proj",
    )(ctx, x, modsel, gains, wqk, wvg, wb, cos, sin)


def _window_attn_kernel(sink_ref, q_ref, kp_ref, ko_ref, kn_ref, kc_ref, vp_ref, vo_ref, vn_ref, vc_ref,
                        o_ref, *, n_ctx_tiles, n_tiles):
    t = pl.program_id(1)
    is_lat = t >= n_ctx_tiles
    big = jnp.int32(1 << 20)
    zero = jnp.int32(0)
    thr_prev = jnp.where(jnp.logical_and(is_lat, t > n_ctx_tiles), zero, big)
    thr_own = jnp.where(is_lat, zero, big)
    thr_next = jnp.where(jnp.logical_and(is_lat, t < n_tiles - 1), zero, big)
    lc = kc_ref.shape[2]
    n_keys = 3 * A_BLOCK + lc
    row = _iota((A_BLOCK, n_keys), 0)
    col = _iota((A_BLOCK, n_keys), 1)
    blk = col // A_BLOCK
    j = col % A_BLOCK
    dist = jnp.where(blk == 0, j - row, jnp.where(blk == 2, row - j, 0))
    thr = jnp.where(blk == 0, thr_prev, jnp.where(blk == 1, thr_own, jnp.where(blk == 2, thr_next, zero)))
    valid = dist >= thr
    group_of_row = _iota((A_GROUP * A_BLOCK, 1), 0) // A_BLOCK
    scores = []
    for hk in range(A_KV_HEADS):
        q = q_ref[0, hk * A_GROUP:(hk + 1) * A_GROUP].reshape(A_GROUP * A_BLOCK, A_HEAD_DIM)
        keys = jnp.concatenate([kp_ref[0, hk], ko_ref[0, hk], kn_ref[0, hk], kc_ref[0, hk]], axis=0)
        scores.append(_mm_nt(q, keys))
    for hk in range(A_KV_HEADS):
        vals = jnp.concatenate([vp_ref[0, hk], vo_ref[0, hk], vn_ref[0, hk], vc_ref[0, hk]], axis=0)
        s = jnp.concatenate([jnp.where(valid, scores[hk][g * A_BLOCK:(g + 1) * A_BLOCK], NEG_INF)
                             for g in range(A_GROUP)], axis=0)
        sink = jnp.full((A_GROUP * A_BLOCK, 1), sink_ref[hk * A_GROUP] * LOG2_E, F32)
        for g in range(1, A_GROUP):
            sink = jnp.where(group_of_row == g, sink_ref[hk * A_GROUP + g] * LOG2_E, sink)
        mx = jnp.maximum(jnp.max(s, axis=1, keepdims=True), sink)
        acc = _mm(jnp.exp2(s - mx), vals)
        o = acc[:, :A_HEAD_DIM] / (acc[:, A_HEAD_DIM:] + jnp.exp2(sink - mx))
        for g in range(A_GROUP):
            h = hk * A_GROUP + g
            o_ref[0, :, h * A_HEAD_DIM:(h + 1) * A_HEAD_DIM] = o[g * A_BLOCK:(g + 1) * A_BLOCK].astype(BF16)


def _window_attn(q, k, v, sink, lc):
    bsz, _, t, _ = q.shape
    n_tiles = t // A_BLOCK
    nct = lc // A_BLOCK
    blk = lambda w, f: pl.BlockSpec((1, A_KV_HEADS, A_BLOCK, w), lambda b, i: (b, 0, f(i), 0))
    prev = lambda w: blk(w, lambda i: jnp.clip(i - 1, nct, n_tiles - 1))
    own = lambda w: blk(w, lambda i: i)
    nxt = lambda w: blk(w, lambda i: jnp.clip(i + 1, nct, n_tiles - 1))
    ctx = lambda w: pl.BlockSpec((1, A_KV_HEADS, lc, w), lambda b, i: (b, 0, 0, 0))
    kw, vw = A_HEAD_DIM, 2 * A_HEAD_DIM
    return pl.pallas_call(
        functools.partial(_window_attn_kernel, n_ctx_tiles=nct, n_tiles=n_tiles),
        grid=(bsz, n_tiles),
        in_specs=[pl.BlockSpec(memory_space=pltpu.SMEM),
                  pl.BlockSpec((1, A_HEADS, A_BLOCK, A_HEAD_DIM), lambda b, i: (b, 0, i, 0)),
                  prev(kw), own(kw), nxt(kw), ctx(kw), prev(vw), own(vw), nxt(vw), ctx(vw)],
        out_specs=pl.BlockSpec((1, A_BLOCK, A_Q_W), lambda b, i: (b, i, 0)),
        out_shape=jax.ShapeDtypeStruct((bsz, t, A_Q_W), BF16),
        compiler_params=_params(("parallel", "parallel")),
        name="window_attn",
    )(sink, q, k, k, k, k, v, v, v, v)


TRI_LEAF = 16


def _block_diag(packed, keep):
    n = packed.shape[1] // packed.shape[0]
    return jnp.concatenate([packed.astype(BF16)] * n, axis=0) * keep


def _unit_tri_inverses(mats, keep):
    c = mats[0].shape[0]
    ri = _iota(mats[0].shape, 0)
    ci = _iota(mats[0].shape, 1) % c

    def same_block(size):
        shift = size.bit_length() - 1
        return (ri >> shift) == (ci >> shift)

    def mm(x, y):
        return jnp.dot(x.astype(BF16), _block_diag(y, keep), preferred_element_type=F32)

    leaf_mask = same_block(TRI_LEAF)
    ds = [jnp.where(leaf_mask, a, 0.0) for a in mats]
    rs = [-d for d in ds]
    dps = ds
    size = 2
    while size < TRI_LEAF:
        dps = [mm(dp, dp) for dp in dps]
        rs = [r + dp + mm(r, dp) for r, dp in zip(rs, dps)]
        size *= 2
    size = TRI_LEAF
    while size < c:
        ring = jnp.logical_and(same_block(2 * size), jnp.logical_not(same_block(size)))
        es = [jnp.where(ring, a, 0.0) for a in mats]
        xs = [e + mm(r, e) for r, e in zip(rs, es)]
        rs = [r - x - mm(x, r) for r, x in zip(rs, xs)]
        size *= 2
    return rs


def _delta_prep_kernel(x_ref, xp_ref, xn_ref, g_ref, cw_ref, alog_ref, dtb_ref,
                       u_ref, wq_ref, akt_ref, egl_ref, *, n_ctx_tiles, n_tiles):
    t = pl.program_id(1)
    c = B_CHUNK
    x = x_ref[0]
    rows = _iota(x.shape, 0)
    has_prev = jnp.logical_and(t != 0, t != n_ctx_tiles)
    has_next = jnp.logical_and(t != n_ctx_tiles - 1, t != n_tiles - 1)
    prev_row = xp_ref[0][SUBLANES - 1:SUBLANES, :] * jnp.where(has_prev, 1.0, 0.0)
    next_row = xn_ref[0][0:1, :] * jnp.where(has_next, 1.0, 0.0)
    xm1 = jnp.where(rows == 0, prev_row, pltpu.roll(x, 1, 0))
    xp1 = jnp.where(rows == SEQ_TILE - 1, next_row, pltpu.roll(x, SEQ_TILE - 1, 0))
    y = _silu(cw_ref[0:1, :] * xm1 + cw_ref[1:2, :] * x + cw_ref[2:3, :] * xp1)

    gates = g_ref[0]
    ch = _iota(gates.shape, 1)
    ld = -jnp.exp(alog_ref[...]) * _softplus(gates + dtb_ref[...])
    beta = _sigmoid(gates)
    tr = _iota((SEQ_TILE, SEQ_TILE), 0)
    tc = _iota((SEQ_TILE, SEQ_TILE), 1)
    same_chunk = (tr // c) == (tc // c)
    cum_fwd = jnp.where(jnp.logical_and(same_chunk, tr >= tc), 1.0, 0.0)
    cum_bwd = jnp.where(jnp.logical_and(same_chunk, tr <= tc), 1.0, 0.0)
    gc = jnp.where(ch < B_HEADS, _mm_f32(cum_fwd, ld), _mm_f32(cum_bwd, ld))
    gct = jnp.transpose(jnp.concatenate([gc, jnp.zeros((SEQ_TILE, LANES - N_GATES), F32)], axis=1))

    incl = (jnp.logical_and(same_chunk, tr >= tc), jnp.logical_and(same_chunk, tr <= tc))
    strict = (jnp.logical_and(same_chunk, tr > tc), jnp.logical_and(same_chunk, tr < tc))
    in_first_chunk = _iota((SEQ_TILE, 1), 0) < c
    low_lanes = _iota((c, SEQ_TILE), 1) < c
    n_sys = 2 * SEQ_TILE
    keep = jnp.where((_iota((n_sys, n_sys), 0) // c) == (_iota((n_sys, n_sys), 1) // c), 1.0, 0.0).astype(BF16)
    systems, rhss = [], []
    for h in range(B_HEADS):
        hs = slice(h * B_HEAD_DIM, (h + 1) * B_HEAD_DIM)
        qa = y[:, h * B_HEAD_DIM:(h + 1) * B_HEAD_DIM]
        ka = y[:, B_W + h * B_HEAD_DIM:B_W + (h + 1) * B_HEAD_DIM]
        va = y[:, 2 * B_W + h * B_HEAD_DIM:2 * B_W + (h + 1) * B_HEAD_DIM]
        qa = qa * (lax.rsqrt(jnp.sum(qa * qa, axis=-1, keepdims=True) + NORM_EPS) * (B_HEAD_DIM ** -0.5))
        ka = ka * lax.rsqrt(jnp.sum(ka * ka, axis=-1, keepdims=True) + NORM_EPS)
        kk = _mm_nt(ka, ka)
        qk = _mm_nt(qa, ka)
        a_dir, rhs_dir = [], []
        for d in range(2):
            chn = d * B_HEADS + h
            gcol = gc[:, chn:chn + 1]
            grow = gct[chn:chn + 1, :]
            bcol = beta[:, 2 * B_HEADS + chn:2 * B_HEADS + chn + 1]
            decay = jnp.exp(jnp.where(incl[d], gcol - grow, NEG_INF))
            a_dir.append(jnp.where(strict[d], kk * decay, 0.0) * bcol)
            eg = jnp.exp(gcol)
            rhs_dir.append(jnp.concatenate([va * bcol, ka * (bcol * eg)], axis=1))
            last = (c - 1, 2 * c - 1) if d == 0 else (0, c)
            gl = [gc[r:r + 1, chn:chn + 1] for r in last]
            q_head = (qa * eg).astype(BF16)
            a_qk = (qk * decay).astype(BF16)
            ktail_t = jnp.transpose(ka * jnp.exp(jnp.where(in_first_chunk, gl[0], gl[1]) - gcol)).astype(BF16)
            for cc in range(SEQ_TILE // c):
                cs = slice(cc * c, (cc + 1) * c)
                wq_ref[0, d, cc, h, c:, :] = q_head[cs]
                akt_ref[0, d, cc, h, :c, :] = a_qk[cs, cs]
                akt_ref[0, d, cc, h, c:, :] = ktail_t[:, cs]
                egl_ref[0, 0, cc, chn:chn + 1, :] = jnp.broadcast_to(jnp.exp(gl[cc]), (1, LANES))
        systems.append(jnp.concatenate([jnp.where(low_lanes, a[:c], a[c:]) for a in a_dir], axis=1))
        rhss.append(jnp.concatenate(rhs_dir, axis=0))
    inverses = _unit_tri_inverses(systems, keep)
    sols = [rhs + jnp.dot(_block_diag(r, keep), rhs.astype(BF16), preferred_element_type=F32)
            for r, rhs in zip(inverses, rhss)]
    for h in range(B_HEADS):
        hs = slice(h * B_HEAD_DIM, (h + 1) * B_HEAD_DIM)
        for d in range(2):
            rows_d = slice(d * SEQ_TILE, (d + 1) * SEQ_TILE)
            u_ref[0, d, :, hs] = sols[h][rows_d, :B_HEAD_DIM]
            w = sols[h][rows_d, B_HEAD_DIM:].astype(BF16)
            for cc in range(SEQ_TILE // c):
                wq_ref[0, d, cc, h, :c, :] = w[cc * c:(cc + 1) * c]


def _delta_prep(qkvb, gates, conv_w, alog, dtb, lc):
    bsz, t, _ = qkvb.shape
    n_tiles = t // SEQ_TILE
    nct = lc // SEQ_TILE
    per_tile = SEQ_TILE // SUBLANES
    n_rows8 = t // SUBLANES
    cpt = SEQ_TILE // B_CHUNK
    n_chunks = t // B_CHUNK
    per_chunk = lambda r, w: pl.BlockSpec((1, 2, cpt, B_HEADS, r, w), lambda b, i: (b, 0, i, 0, 0, 0))
    out_shape = [
        jax.ShapeDtypeStruct((bsz, 2, t, B_W), F32),
        jax.ShapeDtypeStruct((bsz, 2, n_chunks, B_HEADS, 2 * B_CHUNK, B_HEAD_DIM), BF16),
        jax.ShapeDtypeStruct((bsz, 2, n_chunks, B_HEADS, B_CHUNK + B_HEAD_DIM, B_CHUNK), BF16),
        jax.ShapeDtypeStruct((bsz, n_tiles, cpt, 2 * B_HEADS, LANES), F32),
    ]
    out_specs = [
        pl.BlockSpec((1, 2, SEQ_TILE, B_W), lambda b, i: (b, 0, i, 0)),
        per_chunk(2 * B_CHUNK, B_HEAD_DIM), per_chunk(B_CHUNK + B_HEAD_DIM, B_CHUNK),
        pl.BlockSpec((1, 1, cpt, 2 * B_HEADS, LANES), lambda b, i: (b, i, 0, 0, 0)),
    ]
    return pl.pallas_call(
        functools.partial(_delta_prep_kernel, n_ctx_tiles=nct, n_tiles=n_tiles),
        grid=(bsz, n_tiles),
        in_specs=[
            pl.BlockSpec((1, SEQ_TILE, 3 * B_W), lambda b, i: (b, i, 0)),
            pl.BlockSpec((1, SUBLANES, 3 * B_W), lambda b, i: (b, jnp.maximum(i * per_tile - 1, 0), 0)),
            pl.BlockSpec((1, SUBLANES, 3 * B_W), lambda b, i: (b, jnp.minimum((i + 1) * per_tile, n_rows8 - 1), 0)),
            pl.BlockSpec((1, SEQ_TILE, N_GATES), lambda b, i: (b, i, 0)),
            _const_spec(conv_w.shape), _const_spec(alog.shape), _const_spec(dtb.shape),
        ],
        out_specs=out_specs,
        out_shape=out_shape,
        compiler_params=_params(("parallel", "parallel")),
        name="delta_prep",
    )(qkvb, qkvb, qkvb, gates, conv_w, alog, dtb)


def _delta_scan_kernel(uf_ref, wqf_ref, aktf_ref, eglf_ref, ub_ref, wqb_ref, aktb_ref, eglb_ref,
                       of_ref, ob_ref, s_ref):
    c = B_CHUNK
    cpt = SEQ_TILE // c

    @pl.when(pl.program_id(1) == 0)
    def _():
        s_ref[...] = jnp.zeros_like(s_ref)

    dirs = ((uf_ref, wqf_ref, aktf_ref, eglf_ref, of_ref), (ub_ref, wqb_ref, aktb_ref, eglb_ref, ob_ref))
    chains = [(d, h) for d in range(2) for h in range(B_HEADS)]
    for pos in range(cpt):
        chunk_of = (pos, cpt - 1 - pos)
        states = [s_ref[d * B_HEADS + h] for d, h in chains]
        read = [_mm(dirs[d][1][0, 0, chunk_of[d], h], s) for (d, h), s in zip(chains, states)]
        v_new = []
        for (d, h), r in zip(chains, read):
            rs = slice(chunk_of[d] * c, (chunk_of[d] + 1) * c)
            v_new.append(dirs[d][0][0, 0, rs, h * B_HEAD_DIM:(h + 1) * B_HEAD_DIM] - r[:c])
        upd = [_mm(dirs[d][2][0, 0, chunk_of[d], h], v) for (d, h), v in zip(chains, v_new)]
        for (d, h), s, r, x in zip(chains, states, read, upd):
            cc = chunk_of[d]
            chn = d * B_HEADS + h
            dirs[d][4][0, cc * c:(cc + 1) * c, h * B_HEAD_DIM:(h + 1) * B_HEAD_DIM] = r[c:] + x[:c]
            s_ref[chn] = s * dirs[d][3][0, 0, cc, chn:chn + 1, :] + x[c:]


def _delta_scan(u, wq, akt, egl, lc):
    bsz, _, t, _ = u.shape
    n_tiles = t // SEQ_TILE
    nct = lc // SEQ_TILE
    cpt = SEQ_TILE // B_CHUNK

    def tile_of(d):
        if d == 0:
            return lambda i: i
        return lambda i: jnp.where(i < nct, nct - 1 - i, n_tiles - 1 - (i - nct))

    def specs(d):
        f = tile_of(d)
        per_chunk = lambda a: pl.BlockSpec((1, 1, cpt, B_HEADS) + a.shape[4:], lambda b, i: (b, d, f(i), 0, 0, 0))
        return [pl.BlockSpec((1, 1, SEQ_TILE, B_W), lambda b, i: (b, d, f(i), 0)),
                per_chunk(wq), per_chunk(akt),
                pl.BlockSpec((1, 1, cpt, 2 * B_HEADS, LANES), lambda b, i: (b, f(i), 0, 0, 0))]

    out_spec = lambda d: pl.BlockSpec((1, SEQ_TILE, B_W), lambda b, i: (b, tile_of(d)(i), 0))
    return pl.pallas_call(
        _delta_scan_kernel,
        grid=(bsz, n_tiles),
        in_specs=specs(0) + specs(1),
        out_specs=[out_spec(0), out_spec(1)],
        out_shape=[jax.ShapeDtypeStruct((bsz, t, B_W), F32)] * 2,
        scratch_shapes=[pltpu.VMEM((2 * B_HEADS, B_HEAD_DIM, B_HEAD_DIM), F32)],
        compiler_params=_params(("parallel", "arbitrary")),
        name="delta_scan",
    )(u, wq, akt, egl, u, wq, akt, egl)


def _residual_ffn(h, y, m, gains, wg_ref, wu_ref, wd_ref):
    h1 = h + m[2:3] * _rms(y, gains[1:2])
    u2 = (_rms(h1, gains[2:3]) * (1.0 + m[4:5]) + m[3:4]).astype(BF16)
    act = _silu(_mm(u2, wg_ref[...])) * _mm(u2, wu_ref[...])
    f = _mm(act, wd_ref[...])
    return h1 + m[5:6] * _rms(f, gains[3:4])


def _ab_post_kernel(hc_ref, hx_ref, m_ref, g_ref, oa_ref, of_ref, ob_ref, gb_ref, on_ref, wo_ref, wg_ref, wu_ref,
                    wd_ref, out_ref, *, n_ctx_tiles):
    h = _stream_tile(hc_ref, hx_ref, n_ctx_tiles)
    o = of_ref[0] + ob_ref[0]
    gate = _silu(gb_ref[0])
    heads = []
    for hh in range(B_HEADS):
        hs = slice(hh * B_HEAD_DIM, (hh + 1) * B_HEAD_DIM)
        heads.append(_rms(o[:, hs], on_ref[...]) * gate[:, hs])
    obn = jnp.concatenate(heads, axis=1)
    y = _mm(oa_ref[0], wo_ref[:A_Q_W, :]) + _mm(obn, wo_ref[A_Q_W:, :])
    out_ref[0] = _residual_ffn(h, y, m_ref[0, 0], g_ref[...], wg_ref, wu_ref, wd_ref)


def _mla_post_kernel(h_ref, m_ref, g_ref, o_ref, wo_ref, wg_ref, wu_ref, wd_ref, out_ref):
    y = _mm(o_ref[0], wo_ref[...])
    out_ref[0] = _residual_ffn(h_ref[0], y, m_ref[0, 0], g_ref[...], wg_ref, wu_ref, wd_ref)


def _stream_specs(d, n_ctx_tiles):
    tm = TOKEN_TILE
    return [pl.BlockSpec((1, tm, d), lambda b, i: (b, jnp.minimum(i, n_ctx_tiles - 1), 0)),
            pl.BlockSpec((1, tm, d), lambda b, i: (b, jnp.maximum(i - n_ctx_tiles, 0), 0))]


def _stream_tile(ctx_ref, lat_ref, n_ctx_tiles):
    return jnp.where(pl.program_id(1) < n_ctx_tiles, ctx_ref[0], lat_ref[0])


def _post_call(kernel_fn, name, h_inputs, h_specs, h_tile_off, modsel, gains, token_inputs, const_inputs, n_tokens,
               n_ctx_tiles):
    bsz, d = modsel.shape[0], modsel.shape[-1]
    tm = TOKEN_TILE
    resident = lambda a: pl.BlockSpec(a.shape, lambda *_: (0,) * a.ndim, pipeline_mode=pl.Buffered(1))
    in_specs = list(h_specs) + [
        pl.BlockSpec((1, 1, 6, d), lambda b, i: (b, ((i + h_tile_off) >= n_ctx_tiles).astype(jnp.int32), 0, 0)),
        _const_spec(gains.shape),
    ]
    in_specs += [pl.BlockSpec((1, tm, a.shape[-1]), lambda b, i: (b, i, 0)) for a in token_inputs]
    in_specs += [resident(a) if a.size * a.dtype.itemsize > (1 << 20) else _const_spec(a.shape) for a in const_inputs]
    return pl.pallas_call(
        kernel_fn,
        grid=(bsz, n_tokens // tm),
        in_specs=in_specs,
        out_specs=pl.BlockSpec((1, tm, d), lambda b, i: (b, i, 0)),
        out_shape=jax.ShapeDtypeStruct((bsz, n_tokens, d), F32),
        compiler_params=_params(("parallel", "parallel")),
        name=name,
    )(*h_inputs, modsel, gains, *token_inputs, *const_inputs)


def _mla_in_kernel(h_ref, m_ref, g_ref, win_ref, qn_ref, kvn_ref, wqt_ref, wk_ref, wvt_ref,
                   cq_ref, sq_ref, ck_ref, sk_ref, qt_ref, k_ref, vt_ref):
    m = m_ref[0, 0]
    u = _modulate_tile(h_ref[0], m[0:3], g_ref[0:1])
    p = _mm(u, win_ref[...])
    cq = _rms(p[:, :C_Q_RANK], qn_ref[...]).astype(BF16)
    ckv = _rms(p[:, C_Q_RANK:C_Q_RANK + C_KV_RANK], kvn_ref[...]).astype(BF16)
    kpe_tile = p[:, C_Q_RANK + C_KV_RANK:]
    kpe = _rope_lanes(kpe_tile, ck_ref[...], sk_ref[...], C_ROPE // 2)[:, :C_ROPE].astype(BF16)
    qt = _mm_nt(wqt_ref[...], cq)
    kn = _mm(ckv, wk_ref[...])
    vt = _mm_nt(wvt_ref[...], ckv)
    n_nope = C_HEADS * C_NOPE
    half = C_ROPE // 2
    scale = LOG2_E * C_QK ** -0.5
    cos, sin = cq_ref[...], sq_ref[...]
    ones = jnp.ones((C_V, qt.shape[1]), BF16)
    for hh in range(C_HEADS):
        x1 = qt[n_nope + hh * C_ROPE:n_nope + hh * C_ROPE + half]
        x2 = qt[n_nope + hh * C_ROPE + half:n_nope + (hh + 1) * C_ROPE]
        qh = jnp.concatenate([qt[hh * C_NOPE:(hh + 1) * C_NOPE], x1 * cos - x2 * sin, x1 * sin + x2 * cos], axis=0)
        qt_ref[0, hh] = (qh * scale).astype(BF16)
        k_ref[0, hh] = jnp.concatenate([kn[:, hh * C_NOPE:(hh + 1) * C_NOPE].astype(BF16), kpe], axis=1)
        vt_ref[0, hh] = jnp.concatenate([vt[hh * C_V:(hh + 1) * C_V].astype(BF16), ones], axis=0)


def _mla_in(hfull, modsel, gains, win, qn, kvn, wqt, wk, wvt, cos_q, sin_q, cos_k, sin_k, lc):
    bsz, t, d = hfull.shape
    tm = TOKEN_TILE
    nct = lc // tm
    tab = lambda a: pl.BlockSpec((tm, a.shape[1]), lambda b, i: (i, 0))
    tab_t = lambda a: pl.BlockSpec((a.shape[0], tm), lambda b, i: (0, i))
    return pl.pallas_call(
        _mla_in_kernel,
        grid=(bsz, t // tm),
        in_specs=[
            pl.BlockSpec((1, tm, d), lambda b, i: (b, i, 0)),
            pl.BlockSpec((1, 1, 6, d), lambda b, i: (b, (i >= nct).astype(jnp.int32), 0, 0)),
            _const_spec(gains.shape), _const_spec(win.shape), _const_spec(qn.shape), _const_spec(kvn.shape),
            _const_spec(wqt.shape), _const_spec(wk.shape), _const_spec(wvt.shape),
            tab_t(cos_q), tab_t(sin_q), tab(cos_k), tab(sin_k),
        ],
        out_specs=[
            pl.BlockSpec((1, C_HEADS, C_QK, tm), lambda b, i: (b, 0, 0, jnp.maximum(i - nct, 0))),
            pl.BlockSpec((1, C_HEADS, tm, C_QK), lambda b, i: (b, 0, i, 0)),
            pl.BlockSpec((1, C_HEADS, 2 * C_V, tm), lambda b, i: (b, 0, 0, i)),
        ],
        out_shape=[
            jax.ShapeDtypeStruct((bsz, C_HEADS, C_QK, t - lc), BF16),
            jax.ShapeDtypeStruct((bsz, C_HEADS, t, C_QK), BF16),
            jax.ShapeDtypeStruct((bsz, C_HEADS, 2 * C_V, t), BF16),
        ],
        compiler_params=_params(("parallel", "arbitrary")),
        name="mla_in_proj",
    )(hfull, modsel, gains, win, qn, kvn, wqt, wk, wvt, cos_q, sin_q, cos_k, sin_k)


def _flash_kernel(qt_ref, k_ref, vt_ref, o_ref, st_ref, *, n_kv_tiles, tk):
    heads = qt_ref.shape[1]
    tq = qt_ref.shape[3]

    ahead = FLASH_LOOKAHEAD
    assert st_ref.shape[0] == heads and 0 < ahead < heads

    def scores(j, hh):
        start = pl.multiple_of(j * tk, tk)
        st_ref[hh] = jnp.dot(k_ref[0, hh, pl.ds(start, tk), :], qt_ref[0, hh], preferred_element_type=F32)

    def absorb(j, hh, m, acc):
        start = pl.multiple_of(j * tk, tk)
        m_new = jnp.maximum(m, jnp.max(st_ref[hh], axis=0, keepdims=True))
        pt = jnp.exp2(st_ref[hh] - m_new).astype(BF16)
        acc = jnp.exp2(m - m_new) * acc + jnp.dot(vt_ref[0, hh, :, pl.ds(start, tk)], pt,
                                                  preferred_element_type=F32)
        return m_new, acc

    def tile(j, state, last):
        out = list(state)
        for hh in range(heads):
            nxt = hh + ahead
            if nxt < heads:
                scores(j, nxt)
            elif not last:
                scores(j + 1, nxt - heads)
            out[hh] = absorb(j, hh, *out[hh])
        return tuple(out)

    state = tuple((jnp.full((1, tq), NEG_INF, F32), jnp.zeros((2 * C_V, tq), F32)) for _ in range(heads))
    for hh in range(ahead):
        scores(0, hh)
    n_loop = n_kv_tiles - 1
    unroll = FLASH_UNROLL if n_loop % FLASH_UNROLL == 0 else 1

    def trip(i, st):
        for k in range(unroll):
            st = tile(i * unroll + k, st, False)
        return st

    state = lax.fori_loop(0, n_loop // unroll, trip, state)
    final = tile(n_kv_tiles - 1, state, True)
    for hh in range(heads):
        acc = jnp.transpose(final[hh][1])
        o_ref[0, :, hh * C_V:(hh + 1) * C_V] = (acc[:, :C_V] / acc[:, C_V:]).astype(BF16)


def _flash(qt, k, vt):
    bsz, n_heads, _, s_len = qt.shape
    t = k.shape[2]
    tq = min(FLASH_TQ, s_len)
    tk = FLASH_TK if t % FLASH_TK == 0 else LANES
    hp = FLASH_HEADS
    return pl.pallas_call(
        functools.partial(_flash_kernel, n_kv_tiles=t // tk, tk=tk),
        grid=(bsz, n_heads // hp, s_len // tq),
        in_specs=[
            pl.BlockSpec((1, hp, C_QK, tq), lambda b, g, i: (b, g, 0, i)),
            pl.BlockSpec((1, hp, t, C_QK), lambda b, g, i: (b, g, 0, 0)),
            pl.BlockSpec((1, hp, 2 * C_V, t), lambda b, g, i: (b, g, 0, 0)),
        ],
        out_specs=pl.BlockSpec((1, tq, hp * C_V), lambda b, g, i: (b, i, g)),
        out_shape=jax.ShapeDtypeStruct((bsz, s_len, n_heads * C_V), BF16),
        scratch_shapes=[pltpu.VMEM((hp, tk, tq), F32)],
        compiler_params=_params(("parallel", "parallel", "arbitrary")),
        name="mla_flash",
    )(qt, k, vt)


def _axial_angles(rows, rot_dim):
    n_freq = rot_dim // 4
    inv_freq = ROPE_BASE ** (-jnp.arange(n_freq, dtype=F32) / n_freq)
    row = jnp.repeat(jnp.arange(rows, dtype=F32), GRID_W)
    col = jnp.tile(jnp.arange(GRID_W, dtype=F32), rows)
    return jnp.concatenate([row[:, None] * inv_freq, col[:, None] * inv_freq], axis=-1)


def _rope_tables(rows, rot_dim, lc, n_groups):
    ang = _axial_angles(rows, rot_dim)
    cos = jnp.concatenate([jnp.cos(ang), jnp.cos(ang)], axis=-1)
    sin = jnp.concatenate([-jnp.sin(ang), jnp.sin(ang)], axis=-1)
    cos = jnp.concatenate([jnp.ones((lc, rot_dim), F32), cos], axis=0)
    sin = jnp.concatenate([jnp.zeros((lc, rot_dim), F32), sin], axis=0)
    return jnp.tile(cos, (1, n_groups)), jnp.tile(sin, (1, n_groups))


def kernel(x, c, ctx, c_ctx, mod_w, mod_b, norm_gains, ffn_w_gate, ffn_w_up, ffn_w_down, ab_w_in, ab_sink, ab_conv, ab_a_log, ab_dt_bias, ab_out_norm, ab_w_out, mla_w_in, mla_q_norm, mla_kv_norm, mla_w_qb, mla_w_kvb, mla_w_out):
    bsz, s_len, d = x.shape
    lc = ctx.shape[1]
    rows = s_len // GRID_W
    assert d == D_MODEL and mod_w.shape[0] == 2
    assert lc % TOKEN_TILE == 0 and s_len % FLASH_TQ == 0 and s_len % TOKEN_TILE == 0
    n_ctx_tiles = lc // TOKEN_TILE


    n_cond = -(-(bsz + 1) // SUBLANES) * SUBLANES
    cond = jnp.concatenate([c, c_ctx[None], jnp.zeros((n_cond - bsz - 1, d), F32)], axis=0)
    mods = _modulation(cond, mod_w, mod_b)

    def mod_select(layer):
        lat = mods[layer, :bsz].reshape(bsz, 1, 6, d)
        cx = jnp.broadcast_to(mods[layer, bsz].reshape(1, 1, 6, d), (bsz, 1, 6, d))
        return jnp.concatenate([cx, lat], axis=1)

    wg = [w.astype(BF16) for w in ffn_w_gate]
    wu = [w.astype(BF16) for w in ffn_w_up]
    wd = [w.astype(BF16) for w in ffn_w_down]

    ms0 = mod_select(0)
    w_in = ab_w_in[0]
    o_q, o_k, o_v, o_b, o_g, o_dec = 0, A_Q_W, A_Q_W + A_KV_W, A_Q_W + 2 * A_KV_W, A_Q_W + 2 * A_KV_W + 3 * B_W, A_Q_W + 2 * A_KV_W + 4 * B_W
    wqk = w_in[:, o_q:o_v].astype(BF16)
    wvg = jnp.concatenate([w_in[:, o_v:o_b], w_in[:, o_dec:], jnp.zeros((d, LANES - N_GATES), F32)], axis=1).astype(BF16)
    wb = w_in[:, o_b:o_dec].astype(BF16)
    cos_a, sin_a = _rope_tables(rows, A_HEAD_DIM, lc, A_HEADS + A_KV_HEADS)
    qa, ka, va, qkvb, gb, gates = _ab_in(ctx, x, ms0, norm_gains[0], wqk, wvg, wb, cos_a, sin_a, n_ctx_tiles)

    oa = _window_attn(qa, ka, va, ab_sink[0], lc)

    zeros8 = jnp.zeros((2 * B_HEADS,), F32)
    alog = jnp.concatenate([ab_a_log[0].reshape(-1), zeros8]).reshape(1, N_GATES)
    dtb = jnp.concatenate([ab_dt_bias[0].reshape(-1), zeros8]).reshape(1, N_GATES)
    u, wq, akt, egl = _delta_prep(qkvb, gates, ab_conv[0], alog, dtb, lc)
    o_f, o_bw = _delta_scan(u, wq, akt, egl, lc)

    h1 = _post_call(functools.partial(_ab_post_kernel, n_ctx_tiles=n_ctx_tiles), "ab_post_ffn",
                    [ctx, x], _stream_specs(d, n_ctx_tiles), 0, ms0, norm_gains[0],
                    [oa, o_f, o_bw, gb],
                    [ab_out_norm[0].reshape(1, B_HEAD_DIM), ab_w_out[0].astype(BF16), wg[0], wu[0], wd[0]],
                    lc + s_len, n_ctx_tiles)

    ms1 = mod_select(1)
    win = jnp.concatenate([mla_w_in[0], jnp.zeros((d, C_IN_PAD - mla_w_in.shape[2]), F32)], axis=1).astype(BF16)
    wqb = mla_w_qb[0].reshape(C_Q_RANK, C_HEADS, C_QK)
    wqt = jnp.concatenate([wqb[:, :, :C_NOPE].reshape(C_Q_RANK, -1), wqb[:, :, C_NOPE:].reshape(C_Q_RANK, -1)], axis=1).T.astype(BF16)
    wkvb = mla_w_kvb[0].reshape(C_KV_RANK, C_HEADS, C_NOPE + C_V)
    wk = wkvb[:, :, :C_NOPE].reshape(C_KV_RANK, -1).astype(BF16)
    wvt = wkvb[:, :, C_NOPE:].reshape(C_KV_RANK, -1).T.astype(BF16)
    ang = _axial_angles(rows, C_ROPE)
    cos_q = jnp.concatenate([jnp.ones((lc, C_ROPE // 2), F32), jnp.cos(ang)], axis=0).T
    sin_q = jnp.concatenate([jnp.zeros((lc, C_ROPE // 2), F32), jnp.sin(ang)], axis=0).T
    cos_k, sin_k = _rope_tables(rows, C_ROPE, lc, 1)
    pad = LANES - C_ROPE
    cos_k = jnp.concatenate([cos_k, jnp.ones((lc + s_len, pad), F32)], axis=1)
    sin_k = jnp.concatenate([sin_k, jnp.zeros((lc + s_len, pad), F32)], axis=1)
    qt, k, vt = _mla_in(h1, ms1, norm_gains[1], win, mla_q_norm[0].reshape(1, -1), mla_kv_norm[0].reshape(1, -1),
                        wqt, wk, wvt, cos_q, sin_q, cos_k, sin_k, lc)
    o = _flash(qt, k, vt)

    h1_latent = pl.BlockSpec((1, TOKEN_TILE, d), lambda b, i: (b, i + n_ctx_tiles, 0))
    return _post_call(_mla_post_kernel, "mla_post_ffn", [h1], [h1_latent], n_ctx_tiles, ms1, norm_gains[1],
                      [o], [mla_w_out[0].astype(BF16), wg[1], wu[1], wd[1]], s_len, n_ctx_tiles)
```

```python
import functools

import jax
import jax.numpy as jnp
from jax import lax
from jax.experimental import pallas as pl
from jax.experimental.pallas import tpu as pltpu

F32 = jnp.float32
BF16 = jnp.bfloat16

D_MODEL = 1024
GRID_W = 64
NORM_EPS = 1e-6
ROPE_BASE = 10000.0
NEG_INF = -1e30

A_HEADS = 8
A_KV_HEADS = 2
A_HEAD_DIM = 64
A_BLOCK = 128
A_GROUP = A_HEADS // A_KV_HEADS

B_HEADS = 4
B_HEAD_DIM = 128
B_CHUNK = 64
B_W = B_HEADS * B_HEAD_DIM
N_GATES = 4 * B_HEADS

C_HEADS = 16
C_NOPE = 64
C_ROPE = 32
C_V = 64
C_QK = C_NOPE + C_ROPE
C_Q_RANK = 384
C_KV_RANK = 256
C_IN_PAD = 768

A_Q_W = A_HEADS * A_HEAD_DIM
A_KV_W = A_KV_HEADS * A_HEAD_DIM

V7X_VMEM_BYTES = 64 * 1024 * 1024
VMEM_LIMIT = V7X_VMEM_BYTES - 8 * 1024 * 1024
LANES = 128
SUBLANES = 8

TOKEN_TILE = 256
SEQ_TILE = 128
FLASH_TQ = 512
FLASH_TK = 768
FLASH_HEADS = 4
FLASH_LOOKAHEAD = 2
FLASH_UNROLL = 2
FLASH_QK_ROWS = 384
LOG2_E = 1.4426950408889634


def _silu(x):
    return x * (1.0 / (1.0 + jnp.exp(-x)))


def _sigmoid(x):
    return 1.0 / (1.0 + jnp.exp(-x))


def _softplus(x):
    return jnp.maximum(x, 0.0) + jnp.log1p(jnp.exp(-jnp.abs(x)))


def _rms(x, gain):
    return x * lax.rsqrt(jnp.mean(x * x, axis=-1, keepdims=True) + NORM_EPS) * gain


def _mm(a, b):
    return jnp.dot(a.astype(BF16), b.astype(BF16), preferred_element_type=F32)


def _mm_nt(a, b):
    return lax.dot_general(a.astype(BF16), b.astype(BF16), (((1,), (1,)), ((), ())),
                           preferred_element_type=F32)


def _mm_f32(a, b):
    return jnp.dot(a, b, preferred_element_type=F32, precision=lax.Precision.HIGHEST)


def _iota(shape, dim):
    return lax.broadcasted_iota(jnp.int32, shape, dim)


def _rope_lanes(x, cos, sin, half):
    lane = _iota(x.shape, 1)
    first = (lane % (2 * half)) < half
    partner = jnp.where(first, pltpu.roll(x, LANES - half, 1), pltpu.roll(x, half, 1))
    return x * cos + partner * sin


def _const_spec(shape):
    return pl.BlockSpec(shape, lambda *_: (0,) * len(shape))


def _params(sem):
    return pltpu.CompilerParams(dimension_semantics=sem, vmem_limit_bytes=VMEM_LIMIT)


def _mod_kernel(cond_ref, w_ref, b_ref, o_ref):
    o_ref[0] = _mm_f32(_silu(cond_ref[...]), w_ref[0]) + b_ref[0]


def _modulation(cond, mod_w, mod_b):
    n_layers, d, n_out = mod_w.shape
    rows = cond.shape[0]
    tn = 1536
    return pl.pallas_call(
        _mod_kernel,
        grid=(n_layers, n_out // tn),
        in_specs=[
            pl.BlockSpec((rows, d), lambda l, j: (0, 0)),
            pl.BlockSpec((1, d, tn), lambda l, j: (l, 0, j)),
            pl.BlockSpec((1, 1, tn), lambda l, j: (l, 0, j)),
        ],
        out_specs=pl.BlockSpec((1, rows, tn), lambda l, j: (l, 0, j)),
        out_shape=jax.ShapeDtypeStruct((n_layers, rows, n_out), F32),
        compiler_params=_params(("parallel", "parallel")),
        name="adaln_modulation",
    )(cond, mod_w, mod_b.reshape(n_layers, 1, n_out))


def _modulate_tile(h, m, gain):
    return _rms(h, gain) * (1.0 + m[1:2]) + m[0:1]


def _ab_in_kernel(hc_ref, hx_ref, m_ref, g_ref, wqk_ref, wvg_ref, wb_ref, cos_ref, sin_ref,
                  q_ref, k_ref, v_ref, qkvb_ref, gb_ref, gates_ref, *, n_ctx_tiles):
    m = m_ref[0, 0]
    u = _modulate_tile(_stream_tile(hc_ref, hx_ref, n_ctx_tiles), m[0:3], g_ref[0:1]).astype(BF16)
    qk = _mm(u, wqk_ref[...])
    per_tile = LANES // A_HEAD_DIM
    q_scale = LOG2_E * A_HEAD_DIM ** -0.5
    for j in range((A_Q_W + A_KV_W) // LANES):
        sl = slice(j * LANES, (j + 1) * LANES)
        r = _rope_lanes(qk[:, sl], cos_ref[:, sl], sin_ref[:, sl], A_HEAD_DIM // 2)
        for i in range(per_tile):
            head = j * per_tile + i
            piece = r[:, i * A_HEAD_DIM:(i + 1) * A_HEAD_DIM]
            if head < A_HEADS:
                q_ref[0, head] = (piece * q_scale).astype(BF16)
            else:
                k_ref[0, head - A_HEADS] = piece.astype(BF16)
    vg = _mm(u, wvg_ref[...])
    ones = jnp.ones((vg.shape[0], A_HEAD_DIM), BF16)
    for hk in range(A_KV_HEADS):
        v_ref[0, hk] = jnp.concatenate([vg[:, hk * A_HEAD_DIM:(hk + 1) * A_HEAD_DIM].astype(BF16), ones], axis=1)
    gates_ref[0] = vg[:, A_KV_W:A_KV_W + N_GATES]
    pb = _mm(u, wb_ref[...])
    qkvb_ref[0] = pb[:, :3 * B_W]
    gb_ref[0] = pb[:, 3 * B_W:]


def _ab_in(ctx, x, modsel, gains, wqk, wvg, wb, cos, sin, n_ctx_tiles):
    bsz, s_len, d = x.shape
    t = ctx.shape[1] + s_len
    tm = TOKEN_TILE
    tok = lambda w: pl.BlockSpec((1, tm, w), lambda b, i: (b, i, 0))
    tab = lambda w: pl.BlockSpec((tm, w), lambda b, i: (i, 0))
    heads = lambda n, w: pl.BlockSpec((1, n, tm, w), lambda b, i: (b, 0, i, 0))
    head_shapes = ((A_HEADS, A_HEAD_DIM), (A_KV_HEADS, A_HEAD_DIM), (A_KV_HEADS, 2 * A_HEAD_DIM))
    out_w = (3 * B_W, B_W, N_GATES)
    return pl.pallas_call(
        functools.partial(_ab_in_kernel, n_ctx_tiles=n_ctx_tiles),
        grid=(bsz, t // tm),
        in_specs=_stream_specs(d, n_ctx_tiles) + [
            pl.BlockSpec((1, 1, 6, d), lambda b, i: (b, (i >= n_ctx_tiles).astype(jnp.int32), 0, 0)),
            _const_spec(gains.shape),
            _const_spec(wqk.shape), _const_spec(wvg.shape), _const_spec(wb.shape),
            tab(cos.shape[1]), tab(sin.shape[1]),
        ],
        out_specs=[heads(n, w) for n, w in head_shapes] + [tok(w) for w in out_w],
        out_shape=([jax.ShapeDtypeStruct((bsz, n, t, w), BF16) for n, w in head_shapes]
                   + [jax.ShapeDtypeStruct((bsz, t, w), F32) for w in out_w]),
        compiler_params=_params(("parallel", "parallel")),
        name="ab_in_proj",
    )(ctx, x, modsel, gains, wqk, wvg, wb, cos, sin)


def _window_attn_kernel(sink_ref, q_ref, kp_ref, ko_ref, kn_ref, kc_ref, vp_ref, vo_ref, vn_ref, vc_ref,
                        o_ref, *, n_ctx_tiles, n_tiles):
    t = pl.program_id(1)
    is_lat = t >= n_ctx_tiles
    big = jnp.int32(1 << 20)
    zero = jnp.int32(0)
    thr_prev = jnp.where(jnp.logical_and(is_lat, t > n_ctx_tiles), zero, big)
    thr_own = jnp.where(is_lat, zero, big)
    thr_next = jnp.where(jnp.logical_and(is_lat, t < n_tiles - 1), zero, big)
    lc = kc_ref.shape[2]
    n_keys = 3 * A_BLOCK + lc
    row = _iota((A_BLOCK, n_keys), 0)
    col = _iota((A_BLOCK, n_keys), 1)
    blk = col // A_BLOCK
    j = col % A_BLOCK
    dist = jnp.where(blk == 0, j - row, jnp.where(blk == 2, row - j, 0))
    thr = jnp.where(blk == 0, thr_prev, jnp.where(blk == 1, thr_own, jnp.where(blk == 2, thr_next, zero)))
    valid = dist >= thr
    group_of_row = _iota((A_GROUP * A_BLOCK, 1), 0) // A_BLOCK
    scores = []
    for hk in range(A_KV_HEADS):
        q = q_ref[0, hk * A_GROUP:(hk + 1) * A_GROUP].reshape(A_GROUP * A_BLOCK, A_HEAD_DIM)
        keys = jnp.concatenate([kp_ref[0, hk], ko_ref[0, hk], kn_ref[0, hk], kc_ref[0, hk]], axis=0)
        scores.append(_mm_nt(q, keys))
    for hk in range(A_KV_HEADS):
        vals = jnp.concatenate([vp_ref[0, hk], vo_ref[0, hk], vn_ref[0, hk], vc_ref[0, hk]], axis=0)
        s = jnp.concatenate([jnp.where(valid, scores[hk][g * A_BLOCK:(g + 1) * A_BLOCK], NEG_INF)
                             for g in range(A_GROUP)], axis=0)
        sink = jnp.full((A_GROUP * A_BLOCK, 1), sink_ref[hk * A_GROUP] * LOG2_E, F32)
        for g in range(1, A_GROUP):
            sink = jnp.where(group_of_row == g, sink_ref[hk * A_GROUP + g] * LOG2_E, sink)
        mx = jnp.maximum(jnp.max(s, axis=1, keepdims=True), sink)
        acc = _mm(jnp.exp2(s - mx), vals)
        o = acc[:, :A_HEAD_DIM] / (acc[:, A_HEAD_DIM:] + jnp.exp2(sink - mx))
        for g in range(A_GROUP):
            h = hk * A_GROUP + g
            o_ref[0, :, h * A_HEAD_DIM:(h + 1) * A_HEAD_DIM] = o[g * A_BLOCK:(g + 1) * A_BLOCK].astype(BF16)


def _window_attn(q, k, v, sink, lc):
    bsz, _, t, _ = q.shape
    n_tiles = t // A_BLOCK
    nct = lc // A_BLOCK
    blk = lambda w, f: pl.BlockSpec((1, A_KV_HEADS, A_BLOCK, w), lambda b, i: (b, 0, f(i), 0))
    prev = lambda w: blk(w, lambda i: jnp.clip(i - 1, nct, n_tiles - 1))
    own = lambda w: blk(w, lambda i: i)
    nxt = lambda w: blk(w, lambda i: jnp.clip(i + 1, nct, n_tiles - 1))
    ctx = lambda w: pl.BlockSpec((1, A_KV_HEADS, lc, w), lambda b, i: (b, 0, 0, 0))
    kw, vw = A_HEAD_DIM, 2 * A_HEAD_DIM
    return pl.pallas_call(
        functools.partial(_window_attn_kernel, n_ctx_tiles=nct, n_tiles=n_tiles),
        grid=(bsz, n_tiles),
        in_specs=[pl.BlockSpec(memory_space=pltpu.SMEM),
                  pl.BlockSpec((1, A_HEADS, A_BLOCK, A_HEAD_DIM), lambda b, i: (b, 0, i, 0)),
                  prev(kw), own(kw), nxt(kw), ctx(kw), prev(vw), own(vw), nxt(vw), ctx(vw)],
        out_specs=pl.BlockSpec((1, A_BLOCK, A_Q_W), lambda b, i: (b, i, 0)),
        out_shape=jax.ShapeDtypeStruct((bsz, t, A_Q_W), BF16),
        compiler_params=_params(("parallel", "parallel")),
        name="window_attn",
    )(sink, q, k, k, k, k, v, v, v, v)


TRI_LEAF = 16


def _block_diag(packed, keep):
    n = packed.shape[1] // packed.shape[0]
    return jnp.concatenate([packed.astype(BF16)] * n, axis=0) * keep


def _unit_tri_inverses(mats, keep):
    c = mats[0].shape[0]
    ri = _iota(mats[0].shape, 0)
    ci = _iota(mats[0].shape, 1) % c

    def same_block(size):
        shift = size.bit_length() - 1
        return (ri >> shift) == (ci >> shift)

    def mm(x, y):
        return jnp.dot(x.astype(BF16), _block_diag(y, keep), preferred_element_type=F32)

    leaf_mask = same_block(TRI_LEAF)
    ds = [jnp.where(leaf_mask, a, 0.0) for a in mats]
    rs = [-d for d in ds]
    dps = ds
    size = 2
    while size < TRI_LEAF:
        dps = [mm(dp, dp) for dp in dps]
        rs = [r + dp + mm(r, dp) for r, dp in zip(rs, dps)]
        size *= 2
    size = TRI_LEAF
    while size < c:
        ring = jnp.logical_and(same_block(2 * size), jnp.logical_not(same_block(size)))
        es = [jnp.where(ring, a, 0.0) for a in mats]
        xs = [e + mm(r, e) for r, e in zip(rs, es)]
        rs = [r - x - mm(x, r) for r, x in zip(rs, xs)]
        size *= 2
    return rs


def _delta_prep_kernel(x_ref, xp_ref, xn_ref, g_ref, cw_ref, alog_ref, dtb_ref,
                       u_ref, wq_ref, akt_ref, egl_ref, *, n_ctx_tiles, n_tiles):
    t = pl.program_id(1)
    c = B_CHUNK
    x = x_ref[0]
    rows = _iota(x.shape, 0)
    has_prev = jnp.logical_and(t != 0, t != n_ctx_tiles)
    has_next = jnp.logical_and(t != n_ctx_tiles - 1, t != n_tiles - 1)
    prev_row = xp_ref[0][SUBLANES - 1:SUBLANES, :] * jnp.where(has_prev, 1.0, 0.0)
    next_row = xn_ref[0][0:1, :] * jnp.where(has_next, 1.0, 0.0)
    xm1 = jnp.where(rows == 0, prev_row, pltpu.roll(x, 1, 0))
    xp1 = jnp.where(rows == SEQ_TILE - 1, next_row, pltpu.roll(x, SEQ_TILE - 1, 0))
    y = _silu(cw_ref[0:1, :] * xm1 + cw_ref[1:2, :] * x + cw_ref[2:3, :] * xp1)

    gates = g_ref[0]
    ch = _iota(gates.shape, 1)
    ld = -jnp.exp(alog_ref[...]) * _softplus(gates + dtb_ref[...])
    beta = _sigmoid(gates)
    tr = _iota((SEQ_TILE, SEQ_TILE), 0)
    tc = _iota((SEQ_TILE, SEQ_TILE), 1)
    same_chunk = (tr // c) == (tc // c)
    cum_fwd = jnp.where(jnp.logical_and(same_chunk, tr >= tc), 1.0, 0.0)
    cum_bwd = jnp.where(jnp.logical_and(same_chunk, tr <= tc), 1.0, 0.0)
    gc = jnp.where(ch < B_HEADS, _mm_f32(cum_fwd, ld), _mm_f32(cum_bwd, ld))
    gct = jnp.transpose(jnp.concatenate([gc, jnp.zeros((SEQ_TILE, LANES - N_GATES), F32)], axis=1))

    incl = (jnp.logical_and(same_chunk, tr >= tc), jnp.logical_and(same_chunk, tr <= tc))
    strict = (jnp.logical_and(same_chunk, tr > tc), jnp.logical_and(same_chunk, tr < tc))
    in_first_chunk = _iota((SEQ_TILE, 1), 0) < c
    low_lanes = _iota((c, SEQ_TILE), 1) < c
    n_sys = 2 * SEQ_TILE
    keep = jnp.where((_iota((n_sys, n_sys), 0) // c) == (_iota((n_sys, n_sys), 1) // c), 1.0, 0.0).astype(BF16)
    systems, rhss = [], []
    for h in range(B_HEADS):
        hs = slice(h * B_HEAD_DIM, (h + 1) * B_HEAD_DIM)
        qa = y[:, h * B_HEAD_DIM:(h + 1) * B_HEAD_DIM]
        ka = y[:, B_W + h * B_HEAD_DIM:B_W + (h + 1) * B_HEAD_DIM]
        va = y[:, 2 * B_W + h * B_HEAD_DIM:2 * B_W + (h + 1) * B_HEAD_DIM]
        qa = qa * (lax.rsqrt(jnp.sum(qa * qa, axis=-1, keepdims=True) + NORM_EPS) * (B_HEAD_DIM ** -0.5))
        ka = ka * lax.rsqrt(jnp.sum(ka * ka, axis=-1, keepdims=True) + NORM_EPS)
        kk = _mm_nt(ka, ka)
        qk = _mm_nt(qa, ka)
        a_dir, rhs_dir = [], []
        for d in range(2):
            chn = d * B_HEADS + h
            gcol = gc[:, chn:chn + 1]
            grow = gct[chn:chn + 1, :]
            bcol = beta[:, 2 * B_HEADS + chn:2 * B_HEADS + chn + 1]
            decay = jnp.exp(jnp.where(incl[d], gcol - grow, NEG_INF))
            a_dir.append(jnp.where(strict[d], kk * decay, 0.0) * bcol)
            eg = jnp.exp(gcol)
            rhs_dir.append(jnp.concatenate([va * bcol, ka * (bcol * eg)], axis=1))
            last = (c - 1, 2 * c - 1) if d == 0 else (0, c)
            gl = [gc[r:r + 1, chn:chn + 1] for r in last]
            q_head = (qa * eg).astype(BF16)
            a_qk = (qk * decay).astype(BF16)
            ktail_t = jnp.transpose(ka * jnp.exp(jnp.where(in_first_chunk, gl[0], gl[1]) - gcol)).astype(BF16)
            for cc in range(SEQ_TILE // c):
                cs = slice(cc * c, (cc + 1) * c)
                wq_ref[0, d, cc, h, c:, :] = q_head[cs]
                akt_ref[0, d, cc, h, :c, :] = a_qk[cs, cs]
                akt_ref[0, d, cc, h, c:, :] = ktail_t[:, cs]
                egl_ref[0, 0, cc, chn:chn + 1, :] = jnp.broadcast_to(jnp.exp(gl[cc]), (1, LANES))
        systems.append(jnp.concatenate([jnp.where(low_lanes, a[:c], a[c:]) for a in a_dir], axis=1))
        rhss.append(jnp.concatenate(rhs_dir, axis=0))
    inverses = _unit_tri_inverses(systems, keep)
    sols = [rhs + jnp.dot(_block_diag(r, keep), rhs.astype(BF16), preferred_element_type=F32)
            for r, rhs in zip(inverses, rhss)]
    for h in range(B_HEADS):
        hs = slice(h * B_HEAD_DIM, (h + 1) * B_HEAD_DIM)
        for d in range(2):
            rows_d = slice(d * SEQ_TILE, (d + 1) * SEQ_TILE)
            u_ref[0, d, :, hs] = sols[h][rows_d, :B_HEAD_DIM]
            w = sols[h][rows_d, B_HEAD_DIM:].astype(BF16)
            for cc in range(SEQ_TILE // c):
                wq_ref[0, d, cc, h, :c, :] = w[cc * c:(cc + 1) * c]


def _delta_prep(qkvb, gates, conv_w, alog, dtb, lc):
    bsz, t, _ = qkvb.shape
    n_tiles = t // SEQ_TILE
    nct = lc // SEQ_TILE
    per_tile = SEQ_TILE // SUBLANES
    n_rows8 = t // SUBLANES
    cpt = SEQ_TILE // B_CHUNK
    n_chunks = t // B_CHUNK
    per_chunk = lambda r, w: pl.BlockSpec((1, 2, cpt, B_HEADS, r, w), lambda b, i: (b, 0, i, 0, 0, 0))
    out_shape = [
        jax.ShapeDtypeStruct((bsz, 2, t, B_W), F32),
        jax.ShapeDtypeStruct((bsz, 2, n_chunks, B_HEADS, 2 * B_CHUNK, B_HEAD_DIM), BF16),
        jax.ShapeDtypeStruct((bsz, 2, n_chunks, B_HEADS, B_CHUNK + B_HEAD_DIM, B_CHUNK), BF16),
        jax.ShapeDtypeStruct((bsz, n_tiles, cpt, 2 * B_HEADS, LANES), F32),
    ]
    out_specs = [
        pl.BlockSpec((1, 2, SEQ_TILE, B_W), lambda b, i: (b, 0, i, 0)),
        per_chunk(2 * B_CHUNK, B_HEAD_DIM), per_chunk(B_CHUNK + B_HEAD_DIM, B_CHUNK),
        pl.BlockSpec((1, 1, cpt, 2 * B_HEADS, LANES), lambda b, i: (b, i, 0, 0, 0)),
    ]
    return pl.pallas_call(
        functools.partial(_delta_prep_kernel, n_ctx_tiles=nct, n_tiles=n_tiles),
        grid=(bsz, n_tiles),
        in_specs=[
            pl.BlockSpec((1, SEQ_TILE, 3 * B_W), lambda b, i: (b, i, 0)),
            pl.BlockSpec((1, SUBLANES, 3 * B_W), lambda b, i: (b, jnp.maximum(i * per_tile - 1, 0), 0)),
            pl.BlockSpec((1, SUBLANES, 3 * B_W), lambda b, i: (b, jnp.minimum((i + 1) * per_tile, n_rows8 - 1), 0)),
            pl.BlockSpec((1, SEQ_TILE, N_GATES), lambda b, i: (b, i, 0)),
            _const_spec(conv_w.shape), _const_spec(alog.shape), _const_spec(dtb.shape),
        ],
        out_specs=out_specs,
        out_shape=out_shape,
        compiler_params=_params(("parallel", "parallel")),
        name="delta_prep",
    )(qkvb, qkvb, qkvb, gates, conv_w, alog, dtb)


def _delta_scan_kernel(uf_ref, wqf_ref, aktf_ref, eglf_ref, ub_ref, wqb_ref, aktb_ref, eglb_ref,
                       of_ref, ob_ref, s_ref):
    c = B_CHUNK
    cpt = SEQ_TILE // c

    @pl.when(pl.program_id(1) == 0)
    def _():
        s_ref[...] = jnp.zeros_like(s_ref)

    dirs = ((uf_ref, wqf_ref, aktf_ref, eglf_ref, of_ref), (ub_ref, wqb_ref, aktb_ref, eglb_ref, ob_ref))
    chains = [(d, h) for d in range(2) for h in range(B_HEADS)]
    for pos in range(cpt):
        chunk_of = (pos, cpt - 1 - pos)
        states = [s_ref[d * B_HEADS + h] for d, h in chains]
        read = [_mm(dirs[d][1][0, 0, chunk_of[d], h], s) for (d, h), s in zip(chains, states)]
        v_new = []
        for (d, h), r in zip(chains, read):
            rs = slice(chunk_of[d] * c, (chunk_of[d] + 1) * c)
            v_new.append(dirs[d][0][0, 0, rs, h * B_HEAD_DIM:(h + 1) * B_HEAD_DIM] - r[:c])
        upd = [_mm(dirs[d][2][0, 0, chunk_of[d], h], v) for (d, h), v in zip(chains, v_new)]
        for (d, h), s, r, x in zip(chains, states, read, upd):
            cc = chunk_of[d]
            chn = d * B_HEADS + h
            dirs[d][4][0, cc * c:(cc + 1) * c, h * B_HEAD_DIM:(h + 1) * B_HEAD_DIM] = r[c:] + x[:c]
            s_ref[chn] = s * dirs[d][3][0, 0, cc, chn:chn + 1, :] + x[c:]


def _delta_scan(u, wq, akt, egl, lc):
    bsz, _, t, _ = u.shape
    n_tiles = t // SEQ_TILE
    nct = lc // SEQ_TILE
    cpt = SEQ_TILE // B_CHUNK

    def tile_of(d):
        if d == 0:
            return lambda i: i
        return lambda i: jnp.where(i < nct, nct - 1 - i, n_tiles - 1 - (i - nct))

    def specs(d):
        f = tile_of(d)
        per_chunk = lambda a: pl.BlockSpec((1, 1, cpt, B_HEADS) + a.shape[4:], lambda b, i: (b, d, f(i), 0, 0, 0))
        return [pl.BlockSpec((1, 1, SEQ_TILE, B_W), lambda b, i: (b, d, f(i), 0)),
                per_chunk(wq), per_chunk(akt),
                pl.BlockSpec((1, 1, cpt, 2 * B_HEADS, LANES), lambda b, i: (b, f(i), 0, 0, 0))]

    out_spec = lambda d: pl.BlockSpec((1, SEQ_TILE, B_W), lambda b, i: (b, tile_of(d)(i), 0))
    return pl.pallas_call(
        _delta_scan_kernel,
        grid=(bsz, n_tiles),
        in_specs=specs(0) + specs(1),
        out_specs=[out_spec(0), out_spec(1)],
        out_shape=[jax.ShapeDtypeStruct((bsz, t, B_W), F32)] * 2,
        scratch_shapes=[pltpu.VMEM((2 * B_HEADS, B_HEAD_DIM, B_HEAD_DIM), F32)],
        compiler_params=_params(("parallel", "arbitrary")),
        name="delta_scan",
    )(u, wq, akt, egl, u, wq, akt, egl)


def _residual_ffn(h, y, m, gains, wg_ref, wu_ref, wd_ref):
    h1 = h + m[2:3] * _rms(y, gains[1:2])
    u2 = (_rms(h1, gains[2:3]) * (1.0 + m[4:5]) + m[3:4]).astype(BF16)
    act = _silu(_mm(u2, wg_ref[...])) * _mm(u2, wu_ref[...])
    f = _mm(act, wd_ref[...])
    return h1 + m[5:6] * _rms(f, gains[3:4])


def _ab_post_kernel(hc_ref, hx_ref, m_ref, g_ref, oa_ref, of_ref, ob_ref, gb_ref, on_ref, wo_ref, wg_ref, wu_ref,
                    wd_ref, out_ref, *, n_ctx_tiles):
    h = _stream_tile(hc_ref, hx_ref, n_ctx_tiles)
    o = of_ref[0] + ob_ref[0]
    gate = _silu(gb_ref[0])
    heads = []
    for hh in range(B_HEADS):
        hs = slice(hh * B_HEAD_DIM, (hh + 1) * B_HEAD_DIM)
        heads.append(_rms(o[:, hs], on_ref[...]) * gate[:, hs])
    obn = jnp.concatenate(heads, axis=1)
    y = _mm(oa_ref[0], wo_ref[:A_Q_W, :]) + _mm(obn, wo_ref[A_Q_W:, :])
    out_ref[0] = _residual_ffn(h, y, m_ref[0, 0], g_ref[...], wg_ref, wu_ref, wd_ref)


def _mla_post_kernel(h_ref, m_ref, g_ref, o_ref, wo_ref, wg_ref, wu_ref, wd_ref, out_ref):
    y = _mm(o_ref[0], wo_ref[...])
    out_ref[0] = _residual_ffn(h_ref[0], y, m_ref[0, 0], g_ref[...], wg_ref, wu_ref, wd_ref)


def _stream_specs(d, n_ctx_tiles):
    tm = TOKEN_TILE
    return [pl.BlockSpec((1, tm, d), lambda b, i: (b, jnp.minimum(i, n_ctx_tiles - 1), 0)),
            pl.BlockSpec((1, tm, d), lambda b, i: (b, jnp.maximum(i - n_ctx_tiles, 0), 0))]


def _stream_tile(ctx_ref, lat_ref, n_ctx_tiles):
    return jnp.where(pl.program_id(1) < n_ctx_tiles, ctx_ref[0], lat_ref[0])


def _post_call(kernel_fn, name, h_inputs, h_specs, h_tile_off, modsel, gains, token_inputs, const_inputs, n_tokens,
               n_ctx_tiles):
    bsz, d = modsel.shape[0], modsel.shape[-1]
    tm = TOKEN_TILE
    resident = lambda a: pl.BlockSpec(a.shape, lambda *_: (0,) * a.ndim, pipeline_mode=pl.Buffered(1))
    in_specs = list(h_specs) + [
        pl.BlockSpec((1, 1, 6, d), lambda b, i: (b, ((i + h_tile_off) >= n_ctx_tiles).astype(jnp.int32), 0, 0)),
        _const_spec(gains.shape),
    ]
    in_specs += [pl.BlockSpec((1, tm, a.shape[-1]), lambda b, i: (b, i, 0)) for a in token_inputs]
    in_specs += [resident(a) if a.size * a.dtype.itemsize > (1 << 20) else _const_spec(a.shape) for a in const_inputs]
    return pl.pallas_call(
        kernel_fn,
        grid=(bsz, n_tokens // tm),
        in_specs=in_specs,
        out_specs=pl.BlockSpec((1, tm, d), lambda b, i: (b, i, 0)),
        out_shape=jax.ShapeDtypeStruct((bsz, n_tokens, d), F32),
        compiler_params=_params(("parallel", "parallel")),
        name=name,
    )(*h_inputs, modsel, gains, *token_inputs, *const_inputs)


def _mla_in_kernel(h_ref, m_ref, g_ref, win_ref, qn_ref, kvn_ref, wqt_ref, wk_ref, wvt_ref,
                   cq_ref, sq_ref, ck_ref, sk_ref, qt_ref, k_ref, vt_ref):
    m = m_ref[0, 0]
    u = _modulate_tile(h_ref[0], m[0:3], g_ref[0:1])
    p = _mm(u, win_ref[...])
    cq = _rms(p[:, :C_Q_RANK], qn_ref[...]).astype(BF16)
    ckv = _rms(p[:, C_Q_RANK:C_Q_RANK + C_KV_RANK], kvn_ref[...]).astype(BF16)
    kpe_tile = p[:, C_Q_RANK + C_KV_RANK:]
    kpe = _rope_lanes(kpe_tile, ck_ref[...], sk_ref[...], C_ROPE // 2)[:, :C_ROPE].astype(BF16)
    qt = _mm_nt(wqt_ref[...], cq)
    kn = _mm(ckv, wk_ref[...])
    vt = _mm_nt(wvt_ref[...], ckv)
    n_nope = C_HEADS * C_NOPE
    half = C_ROPE // 2
    scale = LOG2_E * C_QK ** -0.5
    cos, sin = cq_ref[...], sq_ref[...]
    ones = jnp.ones((C_V, qt.shape[1]), BF16)
    for hh in range(C_HEADS):
        x1 = qt[n_nope + hh * C_ROPE:n_nope + hh * C_ROPE + half]
        x2 = qt[n_nope + hh * C_ROPE + half:n_nope + (hh + 1) * C_ROPE]
        qh = jnp.concatenate([qt[hh * C_NOPE:(hh + 1) * C_NOPE], x1 * cos - x2 * sin, x1 * sin + x2 * cos], axis=0)
        qt_ref[0, hh] = (qh * scale).astype(BF16)
        k_ref[0, hh] = jnp.concatenate([kn[:, hh * C_NOPE:(hh + 1) * C_NOPE].astype(BF16), kpe], axis=1)
        vt_ref[0, hh] = jnp.concatenate([vt[hh * C_V:(hh + 1) * C_V].astype(BF16), ones], axis=0)


def _mla_in(hfull, modsel, gains, win, qn, kvn, wqt, wk, wvt, cos_q, sin_q, cos_k, sin_k, lc):
    bsz, t, d = hfull.shape
    tm = TOKEN_TILE
    nct = lc // tm
    tab = lambda a: pl.BlockSpec((tm, a.shape[1]), lambda b, i: (i, 0))
    tab_t = lambda a: pl.BlockSpec((a.shape[0], tm), lambda b, i: (0, i))
    return pl.pallas_call(
        _mla_in_kernel,
        grid=(bsz, t // tm),
        in_specs=[
            pl.BlockSpec((1, tm, d), lambda b, i: (b, i, 0)),
            pl.BlockSpec((1, 1, 6, d), lambda b, i: (b, (i >= nct).astype(jnp.int32), 0, 0)),
            _const_spec(gains.shape), _const_spec(win.shape), _const_spec(qn.shape), _const_spec(kvn.shape),
            _const_spec(wqt.shape), _const_spec(wk.shape), _const_spec(wvt.shape),
            tab_t(cos_q), tab_t(sin_q), tab(cos_k), tab(sin_k),
        ],
        out_specs=[
            pl.BlockSpec((1, C_HEADS, C_QK, tm), lambda b, i: (b, 0, 0, jnp.maximum(i - nct, 0))),
            pl.BlockSpec((1, C_HEADS, tm, C_QK), lambda b, i: (b, 0, i, 0)),
            pl.BlockSpec((1, C_HEADS, 2 * C_V, tm), lambda b, i: (b, 0, 0, i)),
        ],
        out_shape=[
            jax.ShapeDtypeStruct((bsz, C_HEADS, C_QK, t - lc), BF16),
            jax.ShapeDtypeStruct((bsz, C_HEADS, t, C_QK), BF16),
            jax.ShapeDtypeStruct((bsz, C_HEADS, 2 * C_V, t), BF16),
        ],
        compiler_params=_params(("parallel", "arbitrary")),
        name="mla_in_proj",
    )(hfull, modsel, gains, win, qn, kvn, wqt, wk, wvt, cos_q, sin_q, cos_k, sin_k)


def _flash_kernel(qt_ref, k_ref, vt_ref, o_ref, st_ref, *, n_kv_tiles, tk):
    heads = qt_ref.shape[1]
    tq = qt_ref.shape[3]

    ahead = FLASH_LOOKAHEAD
    assert st_ref.shape[0] == heads and 0 < ahead < heads

    def scores(j, hh):
        rows = FLASH_QK_ROWS if tk % FLASH_QK_ROWS == 0 else tk
        for part in range(tk // rows):
            start = pl.multiple_of(j * tk + part * rows, rows)
            st_ref[hh, part * rows:(part + 1) * rows] = jnp.dot(
                k_ref[0, hh, pl.ds(start, rows), :], qt_ref[0, hh], preferred_element_type=F32)

    def absorb(j, hh, m, acc):
        start = pl.multiple_of(j * tk, tk)
        m_new = jnp.maximum(m, jnp.max(st_ref[hh], axis=0, keepdims=True))
        pt = jnp.exp2(st_ref[hh] - m_new).astype(BF16)
        acc = jnp.exp2(m - m_new) * acc + jnp.dot(vt_ref[0, hh, :, pl.ds(start, tk)], pt,
                                                  preferred_element_type=F32)
        return m_new, acc

    def tile(j, state, last):
        out = list(state)
        for hh in range(heads):
            nxt = hh + ahead
            if nxt < heads:
                scores(j, nxt)
            elif not last:
                scores(j + 1, nxt - heads)
            out[hh] = absorb(j, hh, *out[hh])
        return tuple(out)

    state = tuple((jnp.full((1, tq), NEG_INF, F32), jnp.zeros((2 * C_V, tq), F32)) for _ in range(heads))
    for hh in range(ahead):
        scores(0, hh)
    n_loop = n_kv_tiles - 1
    unroll = FLASH_UNROLL if n_loop % FLASH_UNROLL == 0 else 1

    def trip(i, st):
        for k in range(unroll):
            st = tile(i * unroll + k, st, False)
        return st

    state = lax.fori_loop(0, n_loop // unroll, trip, state)
    final = tile(n_kv_tiles - 1, state, True)
    for hh in range(heads):
        acc = jnp.transpose(final[hh][1])
        o_ref[0, :, hh * C_V:(hh + 1) * C_V] = (acc[:, :C_V] / acc[:, C_V:]).astype(BF16)


def _flash(qt, k, vt):
    bsz, n_heads, _, s_len = qt.shape
    t = k.shape[2]
    tq = min(FLASH_TQ, s_len)
    tk = FLASH_TK if t % FLASH_TK == 0 else LANES
    hp = FLASH_HEADS
    return pl.pallas_call(
        functools.partial(_flash_kernel, n_kv_tiles=t // tk, tk=tk),
        grid=(bsz, n_heads // hp, s_len // tq),
        in_specs=[
            pl.BlockSpec((1, hp, C_QK, tq), lambda b, g, i: (b, g, 0, i)),
            pl.BlockSpec((1, hp, t, C_QK), lambda b, g, i: (b, g, 0, 0)),
            pl.BlockSpec((1, hp, 2 * C_V, t), lambda b, g, i: (b, g, 0, 0)),
        ],
        out_specs=pl.BlockSpec((1, tq, hp * C_V), lambda b, g, i: (b, i, g)),
        out_shape=jax.ShapeDtypeStruct((bsz, s_len, n_heads * C_V), BF16),
        scratch_shapes=[pltpu.VMEM((hp, tk, tq), F32)],
        compiler_params=_params(("parallel", "parallel", "arbitrary")),
        name="mla_flash",
    )(qt, k, vt)


def _axial_angles(rows, rot_dim):
    n_freq = rot_dim // 4
    inv_freq = ROPE_BASE ** (-jnp.arange(n_freq, dtype=F32) / n_freq)
    row = jnp.repeat(jnp.arange(rows, dtype=F32), GRID_W)
    col = jnp.tile(jnp.arange(GRID_W, dtype=F32), rows)
    return jnp.concatenate([row[:, None] * inv_freq, col[:, None] * inv_freq], axis=-1)


def _rope_tables(rows, rot_dim, lc, n_groups):
    ang = _axial_angles(rows, rot_dim)
    cos = jnp.concatenate([jnp.cos(ang), jnp.cos(ang)], axis=-1)
    sin = jnp.concatenate([-jnp.sin(ang), jnp.sin(ang)], axis=-1)
    cos = jnp.concatenate([jnp.ones((lc, rot_dim), F32), cos], axis=0)
    sin = jnp.concatenate([jnp.zeros((lc, rot_dim), F32), sin], axis=0)
    return jnp.tile(cos, (1, n_groups)), jnp.tile(sin, (1, n_groups))


def kernel(x, c, ctx, c_ctx, mod_w, mod_b, norm_gains, ffn_w_gate, ffn_w_up, ffn_w_down, ab_w_in, ab_sink, ab_conv, ab_a_log, ab_dt_bias, ab_out_norm, ab_w_out, mla_w_in, mla_q_norm, mla_kv_norm, mla_w_qb, mla_w_kvb, mla_w_out):
    bsz, s_len, d = x.shape
    lc = ctx.shape[1]
    rows = s_len // GRID_W
    assert d == D_MODEL and mod_w.shape[0] == 2
    assert lc % TOKEN_TILE == 0 and s_len % FLASH_TQ == 0 and s_len % TOKEN_TILE == 0
    n_ctx_tiles = lc // TOKEN_TILE


    n_cond = -(-(bsz + 1) // SUBLANES) * SUBLANES
    cond = jnp.concatenate([c, c_ctx[None], jnp.zeros((n_cond - bsz - 1, d), F32)], axis=0)
    mods = _modulation(cond, mod_w, mod_b)

    def mod_select(layer):
        lat = mods[layer, :bsz].reshape(bsz, 1, 6, d)
        cx = jnp.broadcast_to(mods[layer, bsz].reshape(1, 1, 6, d), (bsz, 1, 6, d))
        return jnp.concatenate([cx, lat], axis=1)

    wg = [w.astype(BF16) for w in ffn_w_gate]
    wu = [w.astype(BF16) for w in ffn_w_up]
    wd = [w.astype(BF16) for w in ffn_w_down]

    ms0 = mod_select(0)
    w_in = ab_w_in[0]
    o_q, o_k, o_v, o_b, o_g, o_dec = 0, A_Q_W, A_Q_W + A_KV_W, A_Q_W + 2 * A_KV_W, A_Q_W + 2 * A_KV_W + 3 * B_W, A_Q_W + 2 * A_KV_W + 4 * B_W
    wqk = w_in[:, o_q:o_v].astype(BF16)
    wvg = jnp.concatenate([w_in[:, o_v:o_b], w_in[:, o_dec:], jnp.zeros((d, LANES - N_GATES), F32)], axis=1).astype(BF16)
    wb = w_in[:, o_b:o_dec].astype(BF16)
    cos_a, sin_a = _rope_tables(rows, A_HEAD_DIM, lc, A_HEADS + A_KV_HEADS)
    qa, ka, va, qkvb, gb, gates = _ab_in(ctx, x, ms0, norm_gains[0], wqk, wvg, wb, cos_a, sin_a, n_ctx_tiles)

    oa = _window_attn(qa, ka, va, ab_sink[0], lc)

    zeros8 = jnp.zeros((2 * B_HEADS,), F32)
    alog = jnp.concatenate([ab_a_log[0].reshape(-1), zeros8]).reshape(1, N_GATES)
    dtb = jnp.concatenate([ab_dt_bias[0].reshape(-1), zeros8]).reshape(1, N_GATES)
    u, wq, akt, egl = _delta_prep(qkvb, gates, ab_conv[0], alog, dtb, lc)
    o_f, o_bw = _delta_scan(u, wq, akt, egl, lc)

    h1 = _post_call(functools.partial(_ab_post_kernel, n_ctx_tiles=n_ctx_tiles), "ab_post_ffn",
                    [ctx, x], _stream_specs(d, n_ctx_tiles), 0, ms0, norm_gains[0],
                    [oa, o_f, o_bw, gb],
                    [ab_out_norm[0].reshape(1, B_HEAD_DIM), ab_w_out[0].astype(BF16), wg[0], wu[0], wd[0]],
                    lc + s_len, n_ctx_tiles)

    ms1 = mod_select(1)
    win = jnp.concatenate([mla_w_in[0], jnp.zeros((d, C_IN_PAD - mla_w_in.shape[2]), F32)], axis=1).astype(BF16)
    wqb = mla_w_qb[0].reshape(C_Q_RANK, C_HEADS, C_QK)
    wqt = jnp.concatenate([wqb[:, :, :C_NOPE].reshape(C_Q_RANK, -1), wqb[:, :, C_NOPE:].reshape(C_Q_RANK, -1)], axis=1).T.astype(BF16)
    wkvb = mla_w_kvb[0].reshape(C_KV_RANK, C_HEADS, C_NOPE + C_V)
    wk = wkvb[:, :, :C_NOPE].reshape(C_KV_RANK, -1).astype(BF16)
    wvt = wkvb[:, :, C_NOPE:].reshape(C_KV_RANK, -1).T.astype(BF16)
    ang = _axial_angles(rows, C_ROPE)
    cos_q = jnp.concatenate([jnp.ones((lc, C_ROPE // 2), F32), jnp.cos(ang)], axis=0).T
    sin_q = jnp.concatenate([jnp.zeros((lc, C_ROPE // 2), F32), jnp.sin(ang)], axis=0).T
    cos_k, sin_k = _rope_tables(rows, C_ROPE, lc, 1)
    pad = LANES - C_ROPE
    cos_k = jnp.concatenate([cos_k, jnp.ones((lc + s_len, pad), F32)], axis=1)
    sin_k = jnp.concatenate([sin_k, jnp.zeros((lc + s_len, pad), F32)], axis=1)
    qt, k, vt = _mla_in(h1, ms1, norm_gains[1], win, mla_q_norm[0].reshape(1, -1), mla_kv_norm[0].reshape(1, -1),
                        wqt, wk, wvt, cos_q, sin_q, cos_k, sin_k, lc)
    o = _flash(qt, k, vt)

    h1_latent = pl.BlockSpec((1, TOKEN_TILE, d), lambda b, i: (b, i + n_ctx_tiles, 0))
    return _post_call(_mla_post_kernel, "mla_post_ffn", [h1], [h1_latent], n_ctx_tiles, ms1, norm_gains[1],
                      [o], [mla_w_out[0].astype(BF16), wg[1], wu[1], wd[1]], s_len, n_ctx_tiles)
```

```python
import functools

import jax
import jax.numpy as jnp
from jax import lax
from jax.experimental import pallas as pl
from jax.experimental.pallas import tpu as pltpu

F32 = jnp.float32
BF16 = jnp.bfloat16

D_MODEL = 1024
GRID_W = 64
NORM_EPS = 1e-6
ROPE_BASE = 10000.0
NEG_INF = -1e30

A_HEADS = 8
A_KV_HEADS = 2
A_HEAD_DIM = 64
A_BLOCK = 128
A_GROUP = A_HEADS // A_KV_HEADS

B_HEADS = 4
B_HEAD_DIM = 128
B_CHUNK = 64
B_W = B_HEADS * B_HEAD_DIM
N_GATES = 4 * B_HEADS

C_HEADS = 16
C_NOPE = 64
C_ROPE = 32
C_V = 64
C_QK = C_NOPE + C_ROPE
C_Q_RANK = 384
C_KV_RANK = 256
C_IN_PAD = 768

A_Q_W = A_HEADS * A_HEAD_DIM
A_KV_W = A_KV_HEADS * A_HEAD_DIM

V7X_VMEM_BYTES = 64 * 1024 * 1024
VMEM_LIMIT = V7X_VMEM_BYTES - 8 * 1024 * 1024
LANES = 128
SUBLANES = 8

TOKEN_TILE = 256
SEQ_TILE = 128
PREP_TILES = 2
FLASH_TQ = 512
FLASH_TK = 768
FLASH_HEADS = 4
FLASH_LOOKAHEAD = 2
FLASH_UNROLL = 2
FLASH_QK_ROWS = 384
LOG2_E = 1.4426950408889634


def _silu(x):
    return x * (1.0 / (1.0 + jnp.exp(-x)))


def _sigmoid(x):
    return 1.0 / (1.0 + jnp.exp(-x))


def _softplus(x):
    return jnp.maximum(x, 0.0) + jnp.log1p(jnp.exp(-jnp.abs(x)))


def _rms(x, gain):
    return x * lax.rsqrt(jnp.mean(x * x, axis=-1, keepdims=True) + NORM_EPS) * gain


def _mm(a, b):
    return jnp.dot(a.astype(BF16), b.astype(BF16), preferred_element_type=F32)


def _mm_nt(a, b):
    return lax.dot_general(a.astype(BF16), b.astype(BF16), (((1,), (1,)), ((), ())),
                           preferred_element_type=F32)


def _mm_f32(a, b):
    return jnp.dot(a, b, preferred_element_type=F32, precision=lax.Precision.HIGHEST)


def _mm_exact_select(x, select):
    hi = x.astype(BF16)
    rest = x - hi.astype(F32)
    mid = rest.astype(BF16)
    lo = (rest - mid.astype(F32)).astype(BF16)
    pick = lambda piece: jnp.dot(piece, select, preferred_element_type=F32)
    return pick(hi) + pick(mid) + pick(lo)


def _iota(shape, dim):
    return lax.broadcasted_iota(jnp.int32, shape, dim)


def _rope_lanes(x, cos, sin, half):
    lane = _iota(x.shape, 1)
    first = (lane % (2 * half)) < half
    partner = jnp.where(first, pltpu.roll(x, LANES - half, 1), pltpu.roll(x, half, 1))
    return x * cos + partner * sin


def _const_spec(shape):
    return pl.BlockSpec(shape, lambda *_: (0,) * len(shape))


def _params(sem):
    return pltpu.CompilerParams(dimension_semantics=sem, vmem_limit_bytes=VMEM_LIMIT)


def _mod_kernel(cond_ref, w_ref, b_ref, o_ref):
    o_ref[0] = _mm_f32(_silu(cond_ref[...]), w_ref[0]) + b_ref[0]


def _modulation(cond, mod_w, mod_b):
    n_layers, d, n_out = mod_w.shape
    rows = cond.shape[0]
    tn = 1536
    return pl.pallas_call(
        _mod_kernel,
        grid=(n_layers, n_out // tn),
        in_specs=[
            pl.BlockSpec((rows, d), lambda l, j: (0, 0)),
            pl.BlockSpec((1, d, tn), lambda l, j: (l, 0, j)),
            pl.BlockSpec((1, 1, tn), lambda l, j: (l, 0, j)),
        ],
        out_specs=pl.BlockSpec((1, rows, tn), lambda l, j: (l, 0, j)),
        out_shape=jax.ShapeDtypeStruct((n_layers, rows, n_out), F32),
        compiler_params=_params(("parallel", "parallel")),
        name="adaln_modulation",
    )(cond, mod_w, mod_b.reshape(n_layers, 1, n_out))


def _modulate_tile(h, m, gain):
    return _rms(h, gain) * (1.0 + m[1:2]) + m[0:1]


def _ab_in_kernel(hc_ref, hx_ref, m_ref, g_ref, wqk_ref, wvg_ref, wb_ref, cos_ref, sin_ref,
                  q_ref, k_ref, v_ref, qkvb_ref, gb_ref, gates_ref, *, n_ctx_tiles):
    m = m_ref[0, 0]
    u = _modulate_tile(_stream_tile(hc_ref, hx_ref, n_ctx_tiles), m[0:3], g_ref[0:1]).astype(BF16)
    qk = _mm(u, wqk_ref[...])
    per_tile = LANES // A_HEAD_DIM
    q_scale = LOG2_E * A_HEAD_DIM ** -0.5
    for j in range((A_Q_W + A_KV_W) // LANES):
        sl = slice(j * LANES, (j + 1) * LANES)
        r = _rope_lanes(qk[:, sl], cos_ref[:, sl], sin_ref[:, sl], A_HEAD_DIM // 2)
        for i in range(per_tile):
            head = j * per_tile + i
            piece = r[:, i * A_HEAD_DIM:(i + 1) * A_HEAD_DIM]
            if head < A_HEADS:
                q_ref[0, head] = (piece * q_scale).astype(BF16)
            else:
                k_ref[0, head - A_HEADS] = piece.astype(BF16)
    vg = _mm(u, wvg_ref[...])
    ones = jnp.ones((vg.shape[0], A_HEAD_DIM), BF16)
    for hk in range(A_KV_HEADS):
        v_ref[0, hk] = jnp.concatenate([vg[:, hk * A_HEAD_DIM:(hk + 1) * A_HEAD_DIM].astype(BF16), ones], axis=1)
    gates_ref[0] = vg[:, A_KV_W:A_KV_W + N_GATES]
    pb = _mm(u, wb_ref[...])
    qkvb_ref[0] = pb[:, :3 * B_W]
    gb_ref[0] = pb[:, 3 * B_W:]


def _ab_in(ctx, x, modsel, gains, wqk, wvg, wb, cos, sin, n_ctx_tiles):
    bsz, s_len, d = x.shape
    t = ctx.shape[1] + s_len
    tm = TOKEN_TILE
    tok = lambda w: pl.BlockSpec((1, tm, w), lambda b, i: (b, i, 0))
    tab = lambda w: pl.BlockSpec((tm, w), lambda b, i: (i, 0))
    heads = lambda n, w: pl.BlockSpec((1, n, tm, w), lambda b, i: (b, 0, i, 0))
    head_shapes = ((A_HEADS, A_HEAD_DIM), (A_KV_HEADS, A_HEAD_DIM), (A_KV_HEADS, 2 * A_HEAD_DIM))
    out_w = (3 * B_W, B_W, N_GATES)
    return pl.pallas_call(
        functools.partial(_ab_in_kernel, n_ctx_tiles=n_ctx_tiles),
        grid=(bsz, t // tm),
        in_specs=_stream_specs(d, n_ctx_tiles) + [
            pl.BlockSpec((1, 1, 6, d), lambda b, i: (b, (i >= n_ctx_tiles).astype(jnp.int32), 0, 0)),
            _const_spec(gains.shape),
            _const_spec(wqk.shape), _const_spec(wvg.shape), _const_spec(wb.shape),
            tab(cos.shape[1]), tab(sin.shape[1]),
        ],
        out_specs=[heads(n, w) for n, w in head_shapes] + [tok(w) for w in out_w],
        out_shape=([jax.ShapeDtypeStruct((bsz, n, t, w), BF16) for n, w in head_shapes]
                   + [jax.ShapeDtypeStruct((bsz, t, w), F32) for w in out_w]),
        compiler_params=_params(("parallel", "parallel")),
        name="ab_in_proj",
    )(ctx, x, modsel, gains, wqk, wvg, wb, cos, sin)


def _window_attn_kernel(sink_ref, q_ref, kp_ref, ko_ref, kn_ref, kc_ref, vp_ref, vo_ref, vn_ref, vc_ref,
                        o_ref, *, n_ctx_tiles, n_tiles):
    t = pl.program_id(1)
    is_lat = t >= n_ctx_tiles
    big = jnp.int32(1 << 20)
    zero = jnp.int32(0)
    thr_prev = jnp.where(jnp.logical_and(is_lat, t > n_ctx_tiles), zero, big)
    thr_own = jnp.where(is_lat, zero, big)
    thr_next = jnp.where(jnp.logical_and(is_lat, t < n_tiles - 1), zero, big)
    lc = kc_ref.shape[2]
    n_keys = 3 * A_BLOCK + lc
    row = _iota((A_BLOCK, n_keys), 0)
    col = _iota((A_BLOCK, n_keys), 1)
    blk = col // A_BLOCK
    j = col % A_BLOCK
    dist = jnp.where(blk == 0, j - row, jnp.where(blk == 2, row - j, 0))
    thr = jnp.where(blk == 0, thr_prev, jnp.where(blk == 1, thr_own, jnp.where(blk == 2, thr_next, zero)))
    valid = dist >= thr
    group_of_row = _iota((A_GROUP * A_BLOCK, 1), 0) // A_BLOCK
    scores = []
    for hk in range(A_KV_HEADS):
        q = q_ref[0, hk * A_GROUP:(hk + 1) * A_GROUP].reshape(A_GROUP * A_BLOCK, A_HEAD_DIM)
        keys = jnp.concatenate([kp_ref[0, hk], ko_ref[0, hk], kn_ref[0, hk], kc_ref[0, hk]], axis=0)
        scores.append(_mm_nt(q, keys))
    for hk in range(A_KV_HEADS):
        vals = jnp.concatenate([vp_ref[0, hk], vo_ref[0, hk], vn_ref[0, hk], vc_ref[0, hk]], axis=0)
        s = jnp.concatenate([jnp.where(valid, scores[hk][g * A_BLOCK:(g + 1) * A_BLOCK], NEG_INF)
                             for g in range(A_GROUP)], axis=0)
        sink = jnp.full((A_GROUP * A_BLOCK, 1), sink_ref[hk * A_GROUP] * LOG2_E, F32)
        for g in range(1, A_GROUP):
            sink = jnp.where(group_of_row == g, sink_ref[hk * A_GROUP + g] * LOG2_E, sink)
        mx = jnp.maximum(jnp.max(s, axis=1, keepdims=True), sink)
        acc = _mm(jnp.exp2(s - mx), vals)
        o = acc[:, :A_HEAD_DIM] / (acc[:, A_HEAD_DIM:] + jnp.exp2(sink - mx))
        for g in range(A_GROUP):
            h = hk * A_GROUP + g
            o_ref[0, :, h * A_HEAD_DIM:(h + 1) * A_HEAD_DIM] = o[g * A_BLOCK:(g + 1) * A_BLOCK].astype(BF16)


def _window_attn(q, k, v, sink, lc):
    bsz, _, t, _ = q.shape
    n_tiles = t // A_BLOCK
    nct = lc // A_BLOCK
    blk = lambda w, f: pl.BlockSpec((1, A_KV_HEADS, A_BLOCK, w), lambda b, i: (b, 0, f(i), 0))
    prev = lambda w: blk(w, lambda i: jnp.clip(i - 1, nct, n_tiles - 1))
    own = lambda w: blk(w, lambda i: i)
    nxt = lambda w: blk(w, lambda i: jnp.clip(i + 1, nct, n_tiles - 1))
    ctx = lambda w: pl.BlockSpec((1, A_KV_HEADS, lc, w), lambda b, i: (b, 0, 0, 0))
    kw, vw = A_HEAD_DIM, 2 * A_HEAD_DIM
    return pl.pallas_call(
        functools.partial(_window_attn_kernel, n_ctx_tiles=nct, n_tiles=n_tiles),
        grid=(bsz, n_tiles),
        in_specs=[pl.BlockSpec(memory_space=pltpu.SMEM),
                  pl.BlockSpec((1, A_HEADS, A_BLOCK, A_HEAD_DIM), lambda b, i: (b, 0, i, 0)),
                  prev(kw), own(kw), nxt(kw), ctx(kw), prev(vw), own(vw), nxt(vw), ctx(vw)],
        out_specs=pl.BlockSpec((1, A_BLOCK, A_Q_W), lambda b, i: (b, i, 0)),
        out_shape=jax.ShapeDtypeStruct((bsz, t, A_Q_W), BF16),
        compiler_params=_params(("parallel", "parallel")),
        name="window_attn",
    )(sink, q, k, k, k, k, v, v, v, v)


TRI_LEAF = 16


def _block_diag(packed, keep):
    n = packed.shape[1] // packed.shape[0]
    return jnp.concatenate([packed.astype(BF16)] * n, axis=0) * keep


def _unit_tri_inverses(mats, keep):
    c = mats[0].shape[0]
    ri = _iota(mats[0].shape, 0)
    ci = _iota(mats[0].shape, 1) % c

    def same_block(size):
        shift = size.bit_length() - 1
        return (ri >> shift) == (ci >> shift)

    def mm(x, y):
        return jnp.dot(x.astype(BF16), _block_diag(y, keep), preferred_element_type=F32)

    leaf_mask = same_block(TRI_LEAF)
    ds = [jnp.where(leaf_mask, a, 0.0) for a in mats]
    rs = [-d for d in ds]
    dps = ds
    size = 2
    while size < TRI_LEAF:
        dps = [mm(dp, dp) for dp in dps]
        rs = [r + dp + mm(r, dp) for r, dp in zip(rs, dps)]
        size *= 2
    size = TRI_LEAF
    while size < c:
        ring = jnp.logical_and(same_block(2 * size), jnp.logical_not(same_block(size)))
        es = [jnp.where(ring, a, 0.0) for a in mats]
        xs = [e + mm(r, e) for r, e in zip(rs, es)]
        rs = [r - x - mm(x, r) for r, x in zip(rs, xs)]
        size *= 2
    return rs


def _delta_prep_kernel(x_ref, xp_ref, xn_ref, g_ref, cw_ref, alog_ref, dtb_ref,
                       u_ref, wq_ref, akt_ref, egl_ref, *, n_ctx_tiles, n_tiles):
    t = pl.program_id(1)
    c = B_CHUNK
    cpt = SEQ_TILE // c
    n_rows = PREP_TILES * SEQ_TILE
    x = x_ref[0]
    rows = _iota(x.shape, 0)
    has_prev = jnp.logical_and(t != 0, t != n_ctx_tiles)
    has_next = jnp.logical_and(t != n_ctx_tiles - 1, t != n_tiles - 1)
    prev_row = xp_ref[0][SUBLANES - 1:SUBLANES, :] * jnp.where(has_prev, 1.0, 0.0)
    next_row = xn_ref[0][0:1, :] * jnp.where(has_next, 1.0, 0.0)
    xm1 = jnp.where(rows == 0, prev_row, pltpu.roll(x, 1, 0))
    xp1 = jnp.where(rows == n_rows - 1, next_row, pltpu.roll(x, n_rows - 1, 0))
    y = _silu(cw_ref[0:1, :] * xm1 + cw_ref[1:2, :] * x + cw_ref[2:3, :] * xp1)

    gates = g_ref[0]
    ch = _iota(gates.shape, 1)
    ld = -jnp.exp(alog_ref[...]) * _softplus(gates + dtb_ref[...])
    beta_all = _sigmoid(gates)
    ar = _iota((n_rows, n_rows), 0)
    ac = _iota((n_rows, n_rows), 1)
    same_chunk_all = (ar // c) == (ac // c)
    cum_fwd = jnp.where(jnp.logical_and(same_chunk_all, ar >= ac), 1.0, 0.0)
    cum_bwd = jnp.where(jnp.logical_and(same_chunk_all, ar <= ac), 1.0, 0.0)
    gc_all = jnp.where(ch < B_HEADS, _mm_f32(cum_fwd, ld), _mm_f32(cum_bwd, ld))
    gct_all = jnp.transpose(jnp.concatenate([gc_all, jnp.zeros((n_rows, LANES - N_GATES), F32)], axis=1))
    cols = jnp.where(ch < 2 * B_HEADS, gc_all, beta_all)
    select = jnp.where(_iota((N_GATES, N_GATES * LANES), 0) == _iota((N_GATES, N_GATES * LANES), 1) // LANES,
                       1.0, 0.0).astype(BF16)
    wide = _mm_exact_select(cols, select)
    lanes_of = lambda chn: slice(chn * LANES, (chn + 1) * LANES)

    tr = _iota((SEQ_TILE, SEQ_TILE), 0)
    tc = _iota((SEQ_TILE, SEQ_TILE), 1)
    same_chunk = (tr // c) == (tc // c)
    incl = (jnp.logical_and(same_chunk, tr >= tc), jnp.logical_and(same_chunk, tr <= tc))
    strict = (jnp.logical_and(same_chunk, tr > tc), jnp.logical_and(same_chunk, tr < tc))
    in_first_chunk = tr < c
    low_lanes = _iota((c, SEQ_TILE), 1) < c
    n_sys = 2 * SEQ_TILE
    keep = jnp.where((_iota((n_sys, n_sys), 0) // c) == (_iota((n_sys, n_sys), 1) // c), 1.0, 0.0).astype(BF16)
    problems = [(s, h) for s in range(PREP_TILES) for h in range(B_HEADS)]
    systems, rhss = [], []
    for s, h in problems:
        ts = slice(s * SEQ_TILE, (s + 1) * SEQ_TILE)
        gc, gct = gc_all[ts], gct_all[:, ts]
        qa = y[ts, h * B_HEAD_DIM:(h + 1) * B_HEAD_DIM]
        ka = y[ts, B_W + h * B_HEAD_DIM:B_W + (h + 1) * B_HEAD_DIM]
        va = y[ts, 2 * B_W + h * B_HEAD_DIM:2 * B_W + (h + 1) * B_HEAD_DIM]
        qa = qa * (lax.rsqrt(jnp.sum(qa * qa, axis=-1, keepdims=True) + NORM_EPS) * (B_HEAD_DIM ** -0.5))
        ka = ka * lax.rsqrt(jnp.sum(ka * ka, axis=-1, keepdims=True) + NORM_EPS)
        kk = _mm_nt(ka, ka)
        qk = _mm_nt(qa, ka)
        a_dir, rhs_dir = [], []
        for d in range(2):
            chn = d * B_HEADS + h
            gcol = wide[ts, lanes_of(chn)]
            grow = gct[chn:chn + 1, :]
            bcol = wide[ts, lanes_of(2 * B_HEADS + chn)]
            decay = jnp.exp(jnp.where(incl[d], gcol - grow, NEG_INF))
            a_dir.append(jnp.where(strict[d], kk * decay, 0.0) * bcol)
            eg = jnp.exp(gcol)
            rhs_dir.append(jnp.concatenate([va * bcol, ka * (bcol * eg)], axis=1))
            last = (c - 1, 2 * c - 1) if d == 0 else (0, c)
            gl = [gcol[r:r + 1, :] for r in last]
            q_head = (qa * eg).astype(BF16)
            a_qk = (qk * decay).astype(BF16)
            ktail_t = jnp.transpose(ka * jnp.exp(jnp.where(in_first_chunk, gl[0], gl[1]) - gcol)).astype(BF16)
            for cc in range(cpt):
                cs = slice(cc * c, (cc + 1) * c)
                wq_ref[0, d, s * cpt + cc, h, c:, :] = q_head[cs]
                akt_ref[0, d, s * cpt + cc, h, :c, :] = a_qk[cs, cs]
                akt_ref[0, d, s * cpt + cc, h, c:, :] = ktail_t[:, cs]
                egl_ref[0, s, cc, chn:chn + 1, :] = jnp.exp(gl[cc])
        systems.append(jnp.concatenate([jnp.where(low_lanes, a[:c], a[c:]) for a in a_dir], axis=1))
        rhss.append(jnp.concatenate(rhs_dir, axis=0))
    inverses = _unit_tri_inverses(systems, keep)
    sols = [rhs + jnp.dot(_block_diag(r, keep), rhs.astype(BF16), preferred_element_type=F32)
            for r, rhs in zip(inverses, rhss)]
    for (s, h), sol in zip(problems, sols):
        hs = slice(h * B_HEAD_DIM, (h + 1) * B_HEAD_DIM)
        for d in range(2):
            rows_d = slice(d * SEQ_TILE, (d + 1) * SEQ_TILE)
            u_ref[0, d, s * SEQ_TILE:(s + 1) * SEQ_TILE, hs] = sol[rows_d, :B_HEAD_DIM]
            w = sol[rows_d, B_HEAD_DIM:].astype(BF16)
            for cc in range(cpt):
                wq_ref[0, d, s * cpt + cc, h, :c, :] = w[cc * c:(cc + 1) * c]


def _delta_prep(qkvb, gates, conv_w, alog, dtb, lc):
    bsz, t, _ = qkvb.shape
    n_tiles = t // SEQ_TILE
    step_rows = PREP_TILES * SEQ_TILE
    assert t % step_rows == 0 and lc % step_rows == 0
    n_steps = t // step_rows
    nct = lc // step_rows
    per_step = step_rows // SUBLANES
    n_rows8 = t // SUBLANES
    cpt = SEQ_TILE // B_CHUNK
    n_chunks = t // B_CHUNK
    per_chunk = lambda r, w: pl.BlockSpec((1, 2, PREP_TILES * cpt, B_HEADS, r, w), lambda b, i: (b, 0, i, 0, 0, 0))
    out_shape = [
        jax.ShapeDtypeStruct((bsz, 2, t, B_W), F32),
        jax.ShapeDtypeStruct((bsz, 2, n_chunks, B_HEADS, 2 * B_CHUNK, B_HEAD_DIM), BF16),
        jax.ShapeDtypeStruct((bsz, 2, n_chunks, B_HEADS, B_CHUNK + B_HEAD_DIM, B_CHUNK), BF16),
        jax.ShapeDtypeStruct((bsz, n_tiles, cpt, 2 * B_HEADS, LANES), F32),
    ]
    out_specs = [
        pl.BlockSpec((1, 2, step_rows, B_W), lambda b, i: (b, 0, i, 0)),
        per_chunk(2 * B_CHUNK, B_HEAD_DIM), per_chunk(B_CHUNK + B_HEAD_DIM, B_CHUNK),
        pl.BlockSpec((1, PREP_TILES, cpt, 2 * B_HEADS, LANES), lambda b, i: (b, i, 0, 0, 0)),
    ]
    return pl.pallas_call(
        functools.partial(_delta_prep_kernel, n_ctx_tiles=nct, n_tiles=n_steps),
        grid=(bsz, n_steps),
        in_specs=[
            pl.BlockSpec((1, step_rows, 3 * B_W), lambda b, i: (b, i, 0)),
            pl.BlockSpec((1, SUBLANES, 3 * B_W), lambda b, i: (b, jnp.maximum(i * per_step - 1, 0), 0)),
            pl.BlockSpec((1, SUBLANES, 3 * B_W), lambda b, i: (b, jnp.minimum((i + 1) * per_step, n_rows8 - 1), 0)),
            pl.BlockSpec((1, step_rows, N_GATES), lambda b, i: (b, i, 0)),
            _const_spec(conv_w.shape), _const_spec(alog.shape), _const_spec(dtb.shape),
        ],
        out_specs=out_specs,
        out_shape=out_shape,
        compiler_params=_params(("parallel", "parallel")),
        name="delta_prep",
    )(qkvb, qkvb, qkvb, gates, conv_w, alog, dtb)


def _delta_scan_kernel(uf_ref, wqf_ref, aktf_ref, eglf_ref, ub_ref, wqb_ref, aktb_ref, eglb_ref,
                       of_ref, ob_ref, s_ref):
    c = B_CHUNK
    cpt = SEQ_TILE // c

    @pl.when(pl.program_id(1) == 0)
    def _():
        s_ref[...] = jnp.zeros_like(s_ref)

    dirs = ((uf_ref, wqf_ref, aktf_ref, eglf_ref, of_ref), (ub_ref, wqb_ref, aktb_ref, eglb_ref, ob_ref))
    chains = [(d, h) for d in range(2) for h in range(B_HEADS)]
    for pos in range(cpt):
        chunk_of = (pos, cpt - 1 - pos)
        states = [s_ref[d * B_HEADS + h] for d, h in chains]
        read = [_mm(dirs[d][1][0, 0, chunk_of[d], h], s) for (d, h), s in zip(chains, states)]
        v_new = []
        for (d, h), r in zip(chains, read):
            rs = slice(chunk_of[d] * c, (chunk_of[d] + 1) * c)
            v_new.append(dirs[d][0][0, 0, rs, h * B_HEAD_DIM:(h + 1) * B_HEAD_DIM] - r[:c])
        upd = [_mm(dirs[d][2][0, 0, chunk_of[d], h], v) for (d, h), v in zip(chains, v_new)]
        for (d, h), s, r, x in zip(chains, states, read, upd):
            cc = chunk_of[d]
            chn = d * B_HEADS + h
            dirs[d][4][0, cc * c:(cc + 1) * c, h * B_HEAD_DIM:(h + 1) * B_HEAD_DIM] = r[c:] + x[:c]
            s_ref[chn] = s * dirs[d][3][0, 0, cc, chn:chn + 1, :] + x[c:]


def _delta_scan(u, wq, akt, egl, lc):
    bsz, _, t, _ = u.shape
    n_tiles = t // SEQ_TILE
    nct = lc // SEQ_TILE
    cpt = SEQ_TILE // B_CHUNK

    def tile_of(d):
        if d == 0:
            return lambda i: i
        return lambda i: jnp.where(i < nct, nct - 1 - i, n_tiles - 1 - (i - nct))

    def specs(d):
        f = tile_of(d)
        per_chunk = lambda a: pl.BlockSpec((1, 1, cpt, B_HEADS) + a.shape[4:], lambda b, i: (b, d, f(i), 0, 0, 0))
        return [pl.BlockSpec((1, 1, SEQ_TILE, B_W), lambda b, i: (b, d, f(i), 0)),
                per_chunk(wq), per_chunk(akt),
                pl.BlockSpec((1, 1, cpt, 2 * B_HEADS, LANES), lambda b, i: (b, f(i), 0, 0, 0))]

    out_spec = lambda d: pl.BlockSpec((1, SEQ_TILE, B_W), lambda b, i: (b, tile_of(d)(i), 0))
    return pl.pallas_call(
        _delta_scan_kernel,
        grid=(bsz, n_tiles),
        in_specs=specs(0) + specs(1),
        out_specs=[out_spec(0), out_spec(1)],
        out_shape=[jax.ShapeDtypeStruct((bsz, t, B_W), F32)] * 2,
        scratch_shapes=[pltpu.VMEM((2 * B_HEADS, B_HEAD_DIM, B_HEAD_DIM), F32)],
        compiler_params=_params(("parallel", "arbitrary")),
        name="delta_scan",
    )(u, wq, akt, egl, u, wq, akt, egl)


def _residual_ffn(h, y, m, gains, wg_ref, wu_ref, wd_ref):
    h1 = h + m[2:3] * _rms(y, gains[1:2])
    u2 = (_rms(h1, gains[2:3]) * (1.0 + m[4:5]) + m[3:4]).astype(BF16)
    act = _silu(_mm(u2, wg_ref[...])) * _mm(u2, wu_ref[...])
    f = _mm(act, wd_ref[...])
    return h1 + m[5:6] * _rms(f, gains[3:4])


def _ab_post_kernel(hc_ref, hx_ref, m_ref, g_ref, oa_ref, of_ref, ob_ref, gb_ref, on_ref, wo_ref, wg_ref, wu_ref,
                    wd_ref, out_ref, *, n_ctx_tiles):
    h = _stream_tile(hc_ref, hx_ref, n_ctx_tiles)
    o = of_ref[0] + ob_ref[0]
    gate = _silu(gb_ref[0])
    heads = []
    for hh in range(B_HEADS):
        hs = slice(hh * B_HEAD_DIM, (hh + 1) * B_HEAD_DIM)
        heads.append(_rms(o[:, hs], on_ref[...]) * gate[:, hs])
    obn = jnp.concatenate(heads, axis=1)
    y = _mm(oa_ref[0], wo_ref[:A_Q_W, :]) + _mm(obn, wo_ref[A_Q_W:, :])
    out_ref[0] = _residual_ffn(h, y, m_ref[0, 0], g_ref[...], wg_ref, wu_ref, wd_ref)


def _mla_post_kernel(h_ref, m_ref, g_ref, o_ref, wo_ref, wg_ref, wu_ref, wd_ref, out_ref):
    y = _mm(o_ref[0], wo_ref[...])
    out_ref[0] = _residual_ffn(h_ref[0], y, m_ref[0, 0], g_ref[...], wg_ref, wu_ref, wd_ref)


def _stream_specs(d, n_ctx_tiles):
    tm = TOKEN_TILE
    return [pl.BlockSpec((1, tm, d), lambda b, i: (b, jnp.minimum(i, n_ctx_tiles - 1), 0)),
            pl.BlockSpec((1, tm, d), lambda b, i: (b, jnp.maximum(i - n_ctx_tiles, 0), 0))]


def _stream_tile(ctx_ref, lat_ref, n_ctx_tiles):
    return jnp.where(pl.program_id(1) < n_ctx_tiles, ctx_ref[0], lat_ref[0])


def _post_call(kernel_fn, name, h_inputs, h_specs, h_tile_off, modsel, gains, token_inputs, const_inputs, n_tokens,
               n_ctx_tiles):
    bsz, d = modsel.shape[0], modsel.shape[-1]
    tm = TOKEN_TILE
    resident = lambda a: pl.BlockSpec(a.shape, lambda *_: (0,) * a.ndim, pipeline_mode=pl.Buffered(1))
    in_specs = list(h_specs) + [
        pl.BlockSpec((1, 1, 6, d), lambda b, i: (b, ((i + h_tile_off) >= n_ctx_tiles).astype(jnp.int32), 0, 0)),
        _const_spec(gains.shape),
    ]
    in_specs += [pl.BlockSpec((1, tm, a.shape[-1]), lambda b, i: (b, i, 0)) for a in token_inputs]
    in_specs += [resident(a) if a.size * a.dtype.itemsize > (1 << 20) else _const_spec(a.shape) for a in const_inputs]
    return pl.pallas_call(
        kernel_fn,
        grid=(bsz, n_tokens // tm),
        in_specs=in_specs,
        out_specs=pl.BlockSpec((1, tm, d), lambda b, i: (b, i, 0)),
        out_shape=jax.ShapeDtypeStruct((bsz, n_tokens, d), F32),
        compiler_params=_params(("parallel", "parallel")),
        name=name,
    )(*h_inputs, modsel, gains, *token_inputs, *const_inputs)


def _mla_in_kernel(h_ref, m_ref, g_ref, win_ref, qn_ref, kvn_ref, wqt_ref, wk_ref, wvt_ref,
                   cq_ref, sq_ref, ck_ref, sk_ref, qt_ref, k_ref, vt_ref):
    m = m_ref[0, 0]
    u = _modulate_tile(h_ref[0], m[0:3], g_ref[0:1])
    p = _mm(u, win_ref[...])
    cq = _rms(p[:, :C_Q_RANK], qn_ref[...]).astype(BF16)
    ckv = _rms(p[:, C_Q_RANK:C_Q_RANK + C_KV_RANK], kvn_ref[...]).astype(BF16)
    kpe_tile = p[:, C_Q_RANK + C_KV_RANK:]
    kpe = _rope_lanes(kpe_tile, ck_ref[...], sk_ref[...], C_ROPE // 2)[:, :C_ROPE].astype(BF16)
    qt = _mm_nt(wqt_ref[...], cq)
    kn = _mm(ckv, wk_ref[...])
    vt = _mm_nt(wvt_ref[...], ckv)
    n_nope = C_HEADS * C_NOPE
    half = C_ROPE // 2
    scale = LOG2_E * C_QK ** -0.5
    cos, sin = cq_ref[...], sq_ref[...]
    ones = jnp.ones((C_V, qt.shape[1]), BF16)
    for hh in range(C_HEADS):
        x1 = qt[n_nope + hh * C_ROPE:n_nope + hh * C_ROPE + half]
        x2 = qt[n_nope + hh * C_ROPE + half:n_nope + (hh + 1) * C_ROPE]
        qh = jnp.concatenate([qt[hh * C_NOPE:(hh + 1) * C_NOPE], x1 * cos - x2 * sin, x1 * sin + x2 * cos], axis=0)
        qt_ref[0, hh] = (qh * scale).astype(BF16)
        k_ref[0, hh] = jnp.concatenate([kn[:, hh * C_NOPE:(hh + 1) * C_NOPE].astype(BF16), kpe], axis=1)
        vt_ref[0, hh] = jnp.concatenate([vt[hh * C_V:(hh + 1) * C_V].astype(BF16), ones], axis=0)


def _mla_in(hfull, modsel, gains, win, qn, kvn, wqt, wk, wvt, cos_q, sin_q, cos_k, sin_k, lc):
    bsz, t, d = hfull.shape
    tm = TOKEN_TILE
    nct = lc // tm
    tab = lambda a: pl.BlockSpec((tm, a.shape[1]), lambda b, i: (i, 0))
    tab_t = lambda a: pl.BlockSpec((a.shape[0], tm), lambda b, i: (0, i))
    return pl.pallas_call(
        _mla_in_kernel,
        grid=(bsz, t // tm),
        in_specs=[
            pl.BlockSpec((1, tm, d), lambda b, i: (b, i, 0)),
            pl.BlockSpec((1, 1, 6, d), lambda b, i: (b, (i >= nct).astype(jnp.int32), 0, 0)),
            _const_spec(gains.shape), _const_spec(win.shape), _const_spec(qn.shape), _const_spec(kvn.shape),
            _const_spec(wqt.shape), _const_spec(wk.shape), _const_spec(wvt.shape),
            tab_t(cos_q), tab_t(sin_q), tab(cos_k), tab(sin_k),
        ],
        out_specs=[
            pl.BlockSpec((1, C_HEADS, C_QK, tm), lambda b, i: (b, 0, 0, jnp.maximum(i - nct, 0))),
            pl.BlockSpec((1, C_HEADS, tm, C_QK), lambda b, i: (b, 0, i, 0)),
            pl.BlockSpec((1, C_HEADS, 2 * C_V, tm), lambda b, i: (b, 0, 0, i)),
        ],
        out_shape=[
            jax.ShapeDtypeStruct((bsz, C_HEADS, C_QK, t - lc), BF16),
            jax.ShapeDtypeStruct((bsz, C_HEADS, t, C_QK), BF16),
            jax.ShapeDtypeStruct((bsz, C_HEADS, 2 * C_V, t), BF16),
        ],
        compiler_params=_params(("parallel", "arbitrary")),
        name="mla_in_proj",
    )(hfull, modsel, gains, win, qn, kvn, wqt, wk, wvt, cos_q, sin_q, cos_k, sin_k)


def _flash_kernel(qt_ref, k_ref, vt_ref, o_ref, st_ref, *, n_kv_tiles, tk):
    heads = qt_ref.shape[1]
    tq = qt_ref.shape[3]

    ahead = FLASH_LOOKAHEAD
    assert st_ref.shape[0] == heads and 0 < ahead < heads

    def scores(j, hh):
        rows = FLASH_QK_ROWS if tk % FLASH_QK_ROWS == 0 else tk
        for part in range(tk // rows):
            start = pl.multiple_of(j * tk + part * rows, rows)
            st_ref[hh, part * rows:(part + 1) * rows] = jnp.dot(
                k_ref[0, hh, pl.ds(start, rows), :], qt_ref[0, hh], preferred_element_type=F32)

    def absorb(j, hh, m, acc):
        start = pl.multiple_of(j * tk, tk)
        m_new = jnp.maximum(m, jnp.max(st_ref[hh], axis=0, keepdims=True))
        pt = jnp.exp2(st_ref[hh] - m_new).astype(BF16)
        acc = jnp.exp2(m - m_new) * acc + jnp.dot(vt_ref[0, hh, :, pl.ds(start, tk)], pt,
                                                  preferred_element_type=F32)
        return m_new, acc

    def tile(j, state, last):
        out = list(state)
        for hh in range(heads):
            nxt = hh + ahead
            if nxt < heads:
                scores(j, nxt)
            elif not last:
                scores(j + 1, nxt - heads)
            out[hh] = absorb(j, hh, *out[hh])
        return tuple(out)

    state = tuple((jnp.full((1, tq), NEG_INF, F32), jnp.zeros((2 * C_V, tq), F32)) for _ in range(heads))
    for hh in range(ahead):
        scores(0, hh)
    n_loop = n_kv_tiles - 1
    unroll = FLASH_UNROLL if n_loop % FLASH_UNROLL == 0 else 1

    def trip(i, st):
        for k in range(unroll):
            st = tile(i * unroll + k, st, False)
        return st

    state = lax.fori_loop(0, n_loop // unroll, trip, state)
    final = tile(n_kv_tiles - 1, state, True)
    for hh in range(heads):
        acc = jnp.transpose(final[hh][1])
        o_ref[0, :, hh * C_V:(hh + 1) * C_V] = (acc[:, :C_V] / acc[:, C_V:]).astype(BF16)


def _flash(qt, k, vt):
    bsz, n_heads, _, s_len = qt.shape
    t = k.shape[2]
    tq = min(FLASH_TQ, s_len)
    tk = FLASH_TK if t % FLASH_TK == 0 else LANES
    hp = FLASH_HEADS
    return pl.pallas_call(
        functools.partial(_flash_kernel, n_kv_tiles=t // tk, tk=tk),
        grid=(bsz, n_heads // hp, s_len // tq),
        in_specs=[
            pl.BlockSpec((1, hp, C_QK, tq), lambda b, g, i: (b, g, 0, i)),
            pl.BlockSpec((1, hp, t, C_QK), lambda b, g, i: (b, g, 0, 0)),
            pl.BlockSpec((1, hp, 2 * C_V, t), lambda b, g, i: (b, g, 0, 0)),
        ],
        out_specs=pl.BlockSpec((1, tq, hp * C_V), lambda b, g, i: (b, i, g)),
        out_shape=jax.ShapeDtypeStruct((bsz, s_len, n_heads * C_V), BF16),
        scratch_shapes=[pltpu.VMEM((hp, tk, tq), F32)],
        compiler_params=_params(("parallel", "parallel", "arbitrary")),
        name="mla_flash",
    )(qt, k, vt)


def _axial_angles(rows, rot_dim):
    n_freq = rot_dim // 4
    inv_freq = ROPE_BASE ** (-jnp.arange(n_freq, dtype=F32) / n_freq)
    row = jnp.repeat(jnp.arange(rows, dtype=F32), GRID_W)
    col = jnp.tile(jnp.arange(GRID_W, dtype=F32), rows)
    return jnp.concatenate([row[:, None] * inv_freq, col[:, None] * inv_freq], axis=-1)


def _rope_tables(rows, rot_dim, lc, n_groups):
    ang = _axial_angles(rows, rot_dim)
    cos = jnp.concatenate([jnp.cos(ang), jnp.cos(ang)], axis=-1)
    sin = jnp.concatenate([-jnp.sin(ang), jnp.sin(ang)], axis=-1)
    cos = jnp.concatenate([jnp.ones((lc, rot_dim), F32), cos], axis=0)
    sin = jnp.concatenate([jnp.zeros((lc, rot_dim), F32), sin], axis=0)
    return jnp.tile(cos, (1, n_groups)), jnp.tile(sin, (1, n_groups))


def kernel(x, c, ctx, c_ctx, mod_w, mod_b, norm_gains, ffn_w_gate, ffn_w_up, ffn_w_down, ab_w_in, ab_sink, ab_conv, ab_a_log, ab_dt_bias, ab_out_norm, ab_w_out, mla_w_in, mla_q_norm, mla_kv_norm, mla_w_qb, mla_w_kvb, mla_w_out):
    bsz, s_len, d = x.shape
    lc = ctx.shape[1]
    rows = s_len // GRID_W
    assert d == D_MODEL and mod_w.shape[0] == 2
    assert lc % TOKEN_TILE == 0 and s_len % FLASH_TQ == 0 and s_len % TOKEN_TILE == 0
    n_ctx_tiles = lc // TOKEN_TILE


    n_cond = -(-(bsz + 1) // SUBLANES) * SUBLANES
    cond = jnp.concatenate([c, c_ctx[None], jnp.zeros((n_cond - bsz - 1, d), F32)], axis=0)
    mods = _modulation(cond, mod_w, mod_b)

    def mod_select(layer):
        lat = mods[layer, :bsz].reshape(bsz, 1, 6, d)
        cx = jnp.broadcast_to(mods[layer, bsz].reshape(1, 1, 6, d), (bsz, 1, 6, d))
        return jnp.concatenate([cx, lat], axis=1)

    wg = [w.astype(BF16) for w in ffn_w_gate]
    wu = [w.astype(BF16) for w in ffn_w_up]
    wd = [w.astype(BF16) for w in ffn_w_down]

    ms0 = mod_select(0)
    w_in = ab_w_in[0]
    o_q, o_k, o_v, o_b, o_g, o_dec = 0, A_Q_W, A_Q_W + A_KV_W, A_Q_W + 2 * A_KV_W, A_Q_W + 2 * A_KV_W + 3 * B_W, A_Q_W + 2 * A_KV_W + 4 * B_W
    wqk = w_in[:, o_q:o_v].astype(BF16)
    wvg = jnp.concatenate([w_in[:, o_v:o_b], w_in[:, o_dec:], jnp.zeros((d, LANES - N_GATES), F32)], axis=1).astype(BF16)
    wb = w_in[:, o_b:o_dec].astype(BF16)
    cos_a, sin_a = _rope_tables(rows, A_HEAD_DIM, lc, A_HEADS + A_KV_HEADS)
    qa, ka, va, qkvb, gb, gates = _ab_in(ctx, x, ms0, norm_gains[0], wqk, wvg, wb, cos_a, sin_a, n_ctx_tiles)

    oa = _window_attn(qa, ka, va, ab_sink[0], lc)

    zeros8 = jnp.zeros((2 * B_HEADS,), F32)
    alog = jnp.concatenate([ab_a_log[0].reshape(-1), zeros8]).reshape(1, N_GATES)
    dtb = jnp.concatenate([ab_dt_bias[0].reshape(-1), zeros8]).reshape(1, N_GATES)
    u, wq, akt, egl = _delta_prep(qkvb, gates, ab_conv[0], alog, dtb, lc)
    o_f, o_bw = _delta_scan(u, wq, akt, egl, lc)

    h1 = _post_call(functools.partial(_ab_post_kernel, n_ctx_tiles=n_ctx_tiles), "ab_post_ffn",
                    [ctx, x], _stream_specs(d, n_ctx_tiles), 0, ms0, norm_gains[0],
                    [oa, o_f, o_bw, gb],
                    [ab_out_norm[0].reshape(1, B_HEAD_DIM), ab_w_out[0].astype(BF16), wg[0], wu[0], wd[0]],
                    lc + s_len, n_ctx_tiles)

    ms1 = mod_select(1)
    win = jnp.concatenate([mla_w_in[0], jnp.zeros((d, C_IN_PAD - mla_w_in.shape[2]), F32)], axis=1).astype(BF16)
    wqb = mla_w_qb[0].reshape(C_Q_RANK, C_HEADS, C_QK)
    wqt = jnp.concatenate([wqb[:, :, :C_NOPE].reshape(C_Q_RANK, -1), wqb[:, :, C_NOPE:].reshape(C_Q_RANK, -1)], axis=1).T.astype(BF16)
    wkvb = mla_w_kvb[0].reshape(C_KV_RANK, C_HEADS, C_NOPE + C_V)
    wk = wkvb[:, :, :C_NOPE].reshape(C_KV_RANK, -1).astype(BF16)
    wvt = wkvb[:, :, C_NOPE:].reshape(C_KV_RANK, -1).T.astype(BF16)
    ang = _axial_angles(rows, C_ROPE)
    cos_q = jnp.concatenate([jnp.ones((lc, C_ROPE // 2), F32), jnp.cos(ang)], axis=0).T
    sin_q = jnp.concatenate([jnp.zeros((lc, C_ROPE // 2), F32), jnp.sin(ang)], axis=0).T
    cos_k, sin_k = _rope_tables(rows, C_ROPE, lc, 1)
    pad = LANES - C_ROPE
    cos_k = jnp.concatenate([cos_k, jnp.ones((lc + s_len, pad), F32)], axis=1)
    sin_k = jnp.concatenate([sin_k, jnp.zeros((lc + s_len, pad), F32)], axis=1)
    qt, k, vt = _mla_in(h1, ms1, norm_gains[1], win, mla_q_norm[0].reshape(1, -1), mla_kv_norm[0].reshape(1, -1),
                        wqt, wk, wvt, cos_q, sin_q, cos_k, sin_k, lc)
    o = _flash(qt, k, vt)

    h1_latent = pl.BlockSpec((1, TOKEN_TILE, d), lambda b, i: (b, i + n_ctx_tiles, 0))
    return _post_call(_mla_post_kernel, "mla_post_ffn", [h1], [h1_latent], n_ctx_tiles, ms1, norm_gains[1],
                      [o], [mla_w_out[0].astype(BF16), wg[1], wu[1], wd[1]], s_len, n_ctx_tiles)
```

```python
import functools

import jax
import jax.numpy as jnp
from jax import lax
from jax.experimental import pallas as pl
from jax.experimental.pallas import tpu as pltpu

F32 = jnp.float32
BF16 = jnp.bfloat16

D_MODEL = 1024
GRID_W = 64
NORM_EPS = 1e-6
ROPE_BASE = 10000.0
NEG_INF = -1e30

A_HEADS = 8
A_KV_HEADS = 2
A_HEAD_DIM = 64
A_BLOCK = 128
A_GROUP = A_HEADS // A_KV_HEADS

B_HEADS = 4
B_HEAD_DIM = 128
B_CHUNK = 64
B_W = B_HEADS * B_HEAD_DIM
N_GATES = 4 * B_HEADS

C_HEADS = 16
C_NOPE = 64
C_ROPE = 32
C_V = 64
C_QK = C_NOPE + C_ROPE
C_Q_RANK = 384
C_KV_RANK = 256
C_IN_PAD = 768

A_Q_W = A_HEADS * A_HEAD_DIM
A_KV_W = A_KV_HEADS * A_HEAD_DIM

V7X_VMEM_BYTES = 64 * 1024 * 1024
VMEM_LIMIT = V7X_VMEM_BYTES - 8 * 1024 * 1024
LANES = 128
SUBLANES = 8

TOKEN_TILE = 256
SEQ_TILE = 128
PREP_TILES = 2
FLASH_TQ = 512
FLASH_TK = 768
FLASH_HEADS = 4
FLASH_LOOKAHEAD = 2
FLASH_UNROLL = 5
FLASH_QK_ROWS = 384
LOG2_E = 1.4426950408889634


def _silu(x):
    return x * (1.0 / (1.0 + jnp.exp(-x)))


def _sigmoid(x):
    return 1.0 / (1.0 + jnp.exp(-x))


def _softplus(x):
    return jnp.maximum(x, 0.0) + jnp.log1p(jnp.exp(-jnp.abs(x)))


def _rms(x, gain):
    return x * lax.rsqrt(jnp.mean(x * x, axis=-1, keepdims=True) + NORM_EPS) * gain


def _mm(a, b):
    return jnp.dot(a.astype(BF16), b.astype(BF16), preferred_element_type=F32)


def _mm_nt(a, b):
    return lax.dot_general(a.astype(BF16), b.astype(BF16), (((1,), (1,)), ((), ())),
                           preferred_element_type=F32)


def _mm_f32(a, b):
    return jnp.dot(a, b, preferred_element_type=F32, precision=lax.Precision.HIGHEST)


def _mm_exact_select(x, select):
    hi = x.astype(BF16)
    rest = x - hi.astype(F32)
    mid = rest.astype(BF16)
    lo = (rest - mid.astype(F32)).astype(BF16)
    pick = lambda piece: jnp.dot(piece, select, preferred_element_type=F32)
    return pick(hi) + pick(mid) + pick(lo)


def _iota(shape, dim):
    return lax.broadcasted_iota(jnp.int32, shape, dim)


def _rope_lanes(x, cos, sin, half):
    lane = _iota(x.shape, 1)
    first = (lane % (2 * half)) < half
    partner = jnp.where(first, pltpu.roll(x, LANES - half, 1), pltpu.roll(x, half, 1))
    return x * cos + partner * sin


def _const_spec(shape):
    return pl.BlockSpec(shape, lambda *_: (0,) * len(shape))


def _params(sem):
    return pltpu.CompilerParams(dimension_semantics=sem, vmem_limit_bytes=VMEM_LIMIT)


def _mod_kernel(cond_ref, w_ref, b_ref, o_ref):
    o_ref[0] = _mm_f32(_silu(cond_ref[...]), w_ref[0]) + b_ref[0]


def _modulation(cond, mod_w, mod_b):
    n_layers, d, n_out = mod_w.shape
    rows = cond.shape[0]
    tn = 1536
    return pl.pallas_call(
        _mod_kernel,
        grid=(n_layers, n_out // tn),
        in_specs=[
            pl.BlockSpec((rows, d), lambda l, j: (0, 0)),
            pl.BlockSpec((1, d, tn), lambda l, j: (l, 0, j)),
            pl.BlockSpec((1, 1, tn), lambda l, j: (l, 0, j)),
        ],
        out_specs=pl.BlockSpec((1, rows, tn), lambda l, j: (l, 0, j)),
        out_shape=jax.ShapeDtypeStruct((n_layers, rows, n_out), F32),
        compiler_params=_params(("parallel", "parallel")),
        name="adaln_modulation",
    )(cond, mod_w, mod_b.reshape(n_layers, 1, n_out))


def _modulate_tile(h, m, gain):
    return _rms(h, gain) * (1.0 + m[1:2]) + m[0:1]


def _ab_in_kernel(hc_ref, hx_ref, m_ref, g_ref, wqk_ref, wvg_ref, wb_ref, cos_ref, sin_ref,
                  q_ref, k_ref, v_ref, qkvb_ref, gb_ref, gates_ref, *, n_ctx_tiles):
    m = m_ref[0, 0]
    u = _modulate_tile(_stream_tile(hc_ref, hx_ref, n_ctx_tiles), m[0:3], g_ref[0:1]).astype(BF16)
    qk = _mm(u, wqk_ref[...])
    per_tile = LANES // A_HEAD_DIM
    q_scale = LOG2_E * A_HEAD_DIM ** -0.5
    for j in range((A_Q_W + A_KV_W) // LANES):
        sl = slice(j * LANES, (j + 1) * LANES)
        r = _rope_lanes(qk[:, sl], cos_ref[:, sl], sin_ref[:, sl], A_HEAD_DIM // 2)
        for i in range(per_tile):
            head = j * per_tile + i
            piece = r[:, i * A_HEAD_DIM:(i + 1) * A_HEAD_DIM]
            if head < A_HEADS:
                q_ref[0, head] = (piece * q_scale).astype(BF16)
            else:
                k_ref[0, head - A_HEADS] = piece.astype(BF16)
    vg = _mm(u, wvg_ref[...])
    ones = jnp.ones((vg.shape[0], A_HEAD_DIM), BF16)
    for hk in range(A_KV_HEADS):
        v_ref[0, hk] = jnp.concatenate([vg[:, hk * A_HEAD_DIM:(hk + 1) * A_HEAD_DIM].astype(BF16), ones], axis=1)
    gates_ref[0] = vg[:, A_KV_W:A_KV_W + N_GATES]
    pb = _mm(u, wb_ref[...])
    qkvb_ref[0] = pb[:, :3 * B_W]
    gb_ref[0] = pb[:, 3 * B_W:]


def _ab_in(ctx, x, modsel, gains, wqk, wvg, wb, cos, sin, n_ctx_tiles):
    bsz, s_len, d = x.shape
    t = ctx.shape[1] + s_len
    tm = TOKEN_TILE
    tok = lambda w: pl.BlockSpec((1, tm, w), lambda b, i: (b, i, 0))
    tab = lambda w: pl.BlockSpec((tm, w), lambda b, i: (i, 0))
    heads = lambda n, w: pl.BlockSpec((1, n, tm, w), lambda b, i: (b, 0, i, 0))
    head_shapes = ((A_HEADS, A_HEAD_DIM), (A_KV_HEADS, A_HEAD_DIM), (A_KV_HEADS, 2 * A_HEAD_DIM))
    out_w = (3 * B_W, B_W, N_GATES)
    return pl.pallas_call(
        functools.partial(_ab_in_kernel, n_ctx_tiles=n_ctx_tiles),
        grid=(bsz, t // tm),
        in_specs=_stream_specs(d, n_ctx_tiles) + [
            pl.BlockSpec((1, 1, 6, d), lambda b, i: (b, (i >= n_ctx_tiles).astype(jnp.int32), 0, 0)),
            _const_spec(gains.shape),
            _const_spec(wqk.shape), _const_spec(wvg.shape), _const_spec(wb.shape),
            tab(cos.shape[1]), tab(sin.shape[1]),
        ],
        out_specs=[heads(n, w) for n, w in head_shapes] + [tok(w) for w in out_w],
        out_shape=([jax.ShapeDtypeStruct((bsz, n, t, w), BF16) for n, w in head_shapes]
                   + [jax.ShapeDtypeStruct((bsz, t, w), F32) for w in out_w]),
        compiler_params=_params(("parallel", "parallel")),
        name="ab_in_proj",
    )(ctx, x, modsel, gains, wqk, wvg, wb, cos, sin)


def _window_attn_kernel(sink_ref, q_ref, kp_ref, ko_ref, kn_ref, kc_ref, vp_ref, vo_ref, vn_ref, vc_ref,
                        o_ref, *, n_ctx_tiles, n_tiles):
    t = pl.program_id(1)
    is_lat = t >= n_ctx_tiles
    big = jnp.int32(1 << 20)
    zero = jnp.int32(0)
    thr_prev = jnp.where(jnp.logical_and(is_lat, t > n_ctx_tiles), zero, big)
    thr_own = jnp.where(is_lat, zero, big)
    thr_next = jnp.where(jnp.logical_and(is_lat, t < n_tiles - 1), zero, big)
    lc = kc_ref.shape[2]
    n_keys = 3 * A_BLOCK + lc
    row = _iota((A_BLOCK, n_keys), 0)
    col = _iota((A_BLOCK, n_keys), 1)
    blk = col // A_BLOCK
    j = col % A_BLOCK
    dist = jnp.where(blk == 0, j - row, jnp.where(blk == 2, row - j, 0))
    thr = jnp.where(blk == 0, thr_prev, jnp.where(blk == 1, thr_own, jnp.where(blk == 2, thr_next, zero)))
    valid = dist >= thr
    group_of_row = _iota((A_GROUP * A_BLOCK, 1), 0) // A_BLOCK
    scores = []
    for hk in range(A_KV_HEADS):
        q = q_ref[0, hk * A_GROUP:(hk + 1) * A_GROUP].reshape(A_GROUP * A_BLOCK, A_HEAD_DIM)
        keys = jnp.concatenate([kp_ref[0, hk], ko_ref[0, hk], kn_ref[0, hk], kc_ref[0, hk]], axis=0)
        scores.append(_mm_nt(q, keys))
    for hk in range(A_KV_HEADS):
        vals = jnp.concatenate([vp_ref[0, hk], vo_ref[0, hk], vn_ref[0, hk], vc_ref[0, hk]], axis=0)
        s = jnp.concatenate([jnp.where(valid, scores[hk][g * A_BLOCK:(g + 1) * A_BLOCK], NEG_INF)
                             for g in range(A_GROUP)], axis=0)
        sink = jnp.full((A_GROUP * A_BLOCK, 1), sink_ref[hk * A_GROUP] * LOG2_E, F32)
        for g in range(1, A_GROUP):
            sink = jnp.where(group_of_row == g, sink_ref[hk * A_GROUP + g] * LOG2_E, sink)
        mx = jnp.maximum(jnp.max(s, axis=1, keepdims=True), sink)
        acc = _mm(jnp.exp2(s - mx), vals)
        o = acc[:, :A_HEAD_DIM] / (acc[:, A_HEAD_DIM:] + jnp.exp2(sink - mx))
        for g in range(A_GROUP):
            h = hk * A_GROUP + g
            o_ref[0, :, h * A_HEAD_DIM:(h + 1) * A_HEAD_DIM] = o[g * A_BLOCK:(g + 1) * A_BLOCK].astype(BF16)


def _window_attn(q, k, v, sink, lc):
    bsz, _, t, _ = q.shape
    n_tiles = t // A_BLOCK
    nct = lc // A_BLOCK
    blk = lambda w, f: pl.BlockSpec((1, A_KV_HEADS, A_BLOCK, w), lambda b, i: (b, 0, f(i), 0))
    prev = lambda w: blk(w, lambda i: jnp.clip(i - 1, nct, n_tiles - 1))
    own = lambda w: blk(w, lambda i: i)
    nxt = lambda w: blk(w, lambda i: jnp.clip(i + 1, nct, n_tiles - 1))
    ctx = lambda w: pl.BlockSpec((1, A_KV_HEADS, lc, w), lambda b, i: (b, 0, 0, 0))
    kw, vw = A_HEAD_DIM, 2 * A_HEAD_DIM
    return pl.pallas_call(
        functools.partial(_window_attn_kernel, n_ctx_tiles=nct, n_tiles=n_tiles),
        grid=(bsz, n_tiles),
        in_specs=[pl.BlockSpec(memory_space=pltpu.SMEM),
                  pl.BlockSpec((1, A_HEADS, A_BLOCK, A_HEAD_DIM), lambda b, i: (b, 0, i, 0)),
                  prev(kw), own(kw), nxt(kw), ctx(kw), prev(vw), own(vw), nxt(vw), ctx(vw)],
        out_specs=pl.BlockSpec((1, A_BLOCK, A_Q_W), lambda b, i: (b, i, 0)),
        out_shape=jax.ShapeDtypeStruct((bsz, t, A_Q_W), BF16),
        compiler_params=_params(("parallel", "parallel")),
        name="window_attn",
    )(sink, q, k, k, k, k, v, v, v, v)


TRI_LEAF = 16


def _block_diag(packed, keep):
    n = packed.shape[1] // packed.shape[0]
    return jnp.concatenate([packed.astype(BF16)] * n, axis=0) * keep


def _unit_tri_inverses(mats, keep):
    c = mats[0].shape[0]
    ri = _iota(mats[0].shape, 0)
    ci = _iota(mats[0].shape, 1) % c

    def same_block(size):
        shift = size.bit_length() - 1
        return (ri >> shift) == (ci >> shift)

    def mm(x, y):
        return jnp.dot(x.astype(BF16), _block_diag(y, keep), preferred_element_type=F32)

    leaf_mask = same_block(TRI_LEAF)
    ds = [jnp.where(leaf_mask, a, 0.0) for a in mats]
    rs = [-d for d in ds]
    dps = ds
    size = 2
    while size < TRI_LEAF:
        dps = [mm(dp, dp) for dp in dps]
        rs = [r + dp + mm(r, dp) for r, dp in zip(rs, dps)]
        size *= 2
    size = TRI_LEAF
    while size < c:
        ring = jnp.logical_and(same_block(2 * size), jnp.logical_not(same_block(size)))
        es = [jnp.where(ring, a, 0.0) for a in mats]
        xs = [e + mm(r, e) for r, e in zip(rs, es)]
        rs = [r - x - mm(x, r) for r, x in zip(rs, xs)]
        size *= 2
    return rs


def _delta_prep_kernel(x_ref, xp_ref, xn_ref, g_ref, cw_ref, alog_ref, dtb_ref,
                       u_ref, wq_ref, akt_ref, egl_ref, *, n_ctx_tiles, n_tiles):
    t = pl.program_id(1)
    c = B_CHUNK
    cpt = SEQ_TILE // c
    n_rows = PREP_TILES * SEQ_TILE
    x = x_ref[0]
    rows = _iota(x.shape, 0)
    has_prev = jnp.logical_and(t != 0, t != n_ctx_tiles)
    has_next = jnp.logical_and(t != n_ctx_tiles - 1, t != n_tiles - 1)
    prev_row = xp_ref[0][SUBLANES - 1:SUBLANES, :] * jnp.where(has_prev, 1.0, 0.0)
    next_row = xn_ref[0][0:1, :] * jnp.where(has_next, 1.0, 0.0)
    xm1 = jnp.where(rows == 0, prev_row, pltpu.roll(x, 1, 0))
    xp1 = jnp.where(rows == n_rows - 1, next_row, pltpu.roll(x, n_rows - 1, 0))
    y = _silu(cw_ref[0:1, :] * xm1 + cw_ref[1:2, :] * x + cw_ref[2:3, :] * xp1)

    gates = g_ref[0]
    ch = _iota(gates.shape, 1)
    ld = -jnp.exp(alog_ref[...]) * _softplus(gates + dtb_ref[...])
    beta_all = _sigmoid(gates)
    ar = _iota((n_rows, n_rows), 0)
    ac = _iota((n_rows, n_rows), 1)
    same_chunk_all = (ar // c) == (ac // c)
    cum_fwd = jnp.where(jnp.logical_and(same_chunk_all, ar >= ac), 1.0, 0.0)
    cum_bwd = jnp.where(jnp.logical_and(same_chunk_all, ar <= ac), 1.0, 0.0)
    gc_all = jnp.where(ch < B_HEADS, _mm_f32(cum_fwd, ld), _mm_f32(cum_bwd, ld))
    gct_all = jnp.transpose(jnp.concatenate([gc_all, jnp.zeros((n_rows, LANES - N_GATES), F32)], axis=1))
    cols = jnp.where(ch < 2 * B_HEADS, gc_all, beta_all)
    select = jnp.where(_iota((N_GATES, N_GATES * LANES), 0) == _iota((N_GATES, N_GATES * LANES), 1) // LANES,
                       1.0, 0.0).astype(BF16)
    wide = _mm_exact_select(cols, select)
    lanes_of = lambda chn: slice(chn * LANES, (chn + 1) * LANES)

    tr = _iota((SEQ_TILE, SEQ_TILE), 0)
    tc = _iota((SEQ_TILE, SEQ_TILE), 1)
    same_chunk = (tr // c) == (tc // c)
    incl = (jnp.logical_and(same_chunk, tr >= tc), jnp.logical_and(same_chunk, tr <= tc))
    strict = (jnp.logical_and(same_chunk, tr > tc), jnp.logical_and(same_chunk, tr < tc))
    in_first_chunk = tr < c
    low_lanes = _iota((c, SEQ_TILE), 1) < c
    n_sys = 2 * SEQ_TILE
    keep = jnp.where((_iota((n_sys, n_sys), 0) // c) == (_iota((n_sys, n_sys), 1) // c), 1.0, 0.0).astype(BF16)
    problems = [(s, h) for s in range(PREP_TILES) for h in range(B_HEADS)]
    systems, rhss = [], []
    for s, h in problems:
        ts = slice(s * SEQ_TILE, (s + 1) * SEQ_TILE)
        gc, gct = gc_all[ts], gct_all[:, ts]
        qa = y[ts, h * B_HEAD_DIM:(h + 1) * B_HEAD_DIM]
        ka = y[ts, B_W + h * B_HEAD_DIM:B_W + (h + 1) * B_HEAD_DIM]
        va = y[ts, 2 * B_W + h * B_HEAD_DIM:2 * B_W + (h + 1) * B_HEAD_DIM]
        qa = qa * (lax.rsqrt(jnp.sum(qa * qa, axis=-1, keepdims=True) + NORM_EPS) * (B_HEAD_DIM ** -0.5))
        ka = ka * lax.rsqrt(jnp.sum(ka * ka, axis=-1, keepdims=True) + NORM_EPS)
        kk = _mm_nt(ka, ka)
        qk = _mm_nt(qa, ka)
        a_dir, rhs_dir = [], []
        for d in range(2):
            chn = d * B_HEADS + h
            gcol = wide[ts, lanes_of(chn)]
            grow = gct[chn:chn + 1, :]
            bcol = wide[ts, lanes_of(2 * B_HEADS + chn)]
            decay = jnp.exp(jnp.where(incl[d], gcol - grow, NEG_INF))
            a_dir.append(jnp.where(strict[d], kk * decay, 0.0) * bcol)
            eg = jnp.exp(gcol)
            rhs_dir.append(jnp.concatenate([va * bcol, ka * (bcol * eg)], axis=1))
            last = (c - 1, 2 * c - 1) if d == 0 else (0, c)
            gl = [gcol[r:r + 1, :] for r in last]
            q_head = (qa * eg).astype(BF16)
            a_qk = (qk * decay).astype(BF16)
            ktail_t = jnp.transpose(ka * jnp.exp(jnp.where(in_first_chunk, gl[0], gl[1]) - gcol)).astype(BF16)
            for cc in range(cpt):
                cs = slice(cc * c, (cc + 1) * c)
                wq_ref[0, d, s * cpt + cc, h, c:, :] = q_head[cs]
                akt_ref[0, d, s * cpt + cc, h, :c, :] = a_qk[cs, cs]
                akt_ref[0, d, s * cpt + cc, h, c:, :] = ktail_t[:, cs]
                egl_ref[0, s, cc, chn:chn + 1, :] = jnp.exp(gl[cc])
        systems.append(jnp.concatenate([jnp.where(low_lanes, a[:c], a[c:]) for a in a_dir], axis=1))
        rhss.append(jnp.concatenate(rhs_dir, axis=0))
    inverses = _unit_tri_inverses(systems, keep)
    sols = [rhs + jnp.dot(_block_diag(r, keep), rhs.astype(BF16), preferred_element_type=F32)
            for r, rhs in zip(inverses, rhss)]
    for (s, h), sol in zip(problems, sols):
        hs = slice(h * B_HEAD_DIM, (h + 1) * B_HEAD_DIM)
        for d in range(2):
            rows_d = slice(d * SEQ_TILE, (d + 1) * SEQ_TILE)
            u_ref[0, d, s * SEQ_TILE:(s + 1) * SEQ_TILE, hs] = sol[rows_d, :B_HEAD_DIM]
            w = sol[rows_d, B_HEAD_DIM:].astype(BF16)
            for cc in range(cpt):
                wq_ref[0, d, s * cpt + cc, h, :c, :] = w[cc * c:(cc + 1) * c]


def _delta_prep(qkvb, gates, conv_w, alog, dtb, lc):
    bsz, t, _ = qkvb.shape
    n_tiles = t // SEQ_TILE
    step_rows = PREP_TILES * SEQ_TILE
    assert t % step_rows == 0 and lc % step_rows == 0
    n_steps = t // step_rows
    nct = lc // step_rows
    per_step = step_rows // SUBLANES
    n_rows8 = t // SUBLANES
    cpt = SEQ_TILE // B_CHUNK
    n_chunks = t // B_CHUNK
    per_chunk = lambda r, w: pl.BlockSpec((1, 2, PREP_TILES * cpt, B_HEADS, r, w), lambda b, i: (b, 0, i, 0, 0, 0))
    out_shape = [
        jax.ShapeDtypeStruct((bsz, 2, t, B_W), F32),
        jax.ShapeDtypeStruct((bsz, 2, n_chunks, B_HEADS, 2 * B_CHUNK, B_HEAD_DIM), BF16),
        jax.ShapeDtypeStruct((bsz, 2, n_chunks, B_HEADS, B_CHUNK + B_HEAD_DIM, B_CHUNK), BF16),
        jax.ShapeDtypeStruct((bsz, n_tiles, cpt, 2 * B_HEADS, LANES), F32),
    ]
    out_specs = [
        pl.BlockSpec((1, 2, step_rows, B_W), lambda b, i: (b, 0, i, 0)),
        per_chunk(2 * B_CHUNK, B_HEAD_DIM), per_chunk(B_CHUNK + B_HEAD_DIM, B_CHUNK),
        pl.BlockSpec((1, PREP_TILES, cpt, 2 * B_HEADS, LANES), lambda b, i: (b, i, 0, 0, 0)),
    ]
    return pl.pallas_call(
        functools.partial(_delta_prep_kernel, n_ctx_tiles=nct, n_tiles=n_steps),
        grid=(bsz, n_steps),
        in_specs=[
            pl.BlockSpec((1, step_rows, 3 * B_W), lambda b, i: (b, i, 0)),
            pl.BlockSpec((1, SUBLANES, 3 * B_W), lambda b, i: (b, jnp.maximum(i * per_step - 1, 0), 0)),
            pl.BlockSpec((1, SUBLANES, 3 * B_W), lambda b, i: (b, jnp.minimum((i + 1) * per_step, n_rows8 - 1), 0)),
            pl.BlockSpec((1, step_rows, N_GATES), lambda b, i: (b, i, 0)),
            _const_spec(conv_w.shape), _const_spec(alog.shape), _const_spec(dtb.shape),
        ],
        out_specs=out_specs,
        out_shape=out_shape,
        compiler_params=_params(("parallel", "parallel")),
        name="delta_prep",
    )(qkvb, qkvb, qkvb, gates, conv_w, alog, dtb)


def _delta_scan_kernel(uf_ref, wqf_ref, aktf_ref, eglf_ref, ub_ref, wqb_ref, aktb_ref, eglb_ref,
                       of_ref, ob_ref, s_ref):
    c = B_CHUNK
    cpt = SEQ_TILE // c

    @pl.when(pl.program_id(1) == 0)
    def _():
        s_ref[...] = jnp.zeros_like(s_ref)

    dirs = ((uf_ref, wqf_ref, aktf_ref, eglf_ref, of_ref), (ub_ref, wqb_ref, aktb_ref, eglb_ref, ob_ref))
    chains = [(d, h) for d in range(2) for h in range(B_HEADS)]
    for pos in range(cpt):
        chunk_of = (pos, cpt - 1 - pos)
        states = [s_ref[d * B_HEADS + h] for d, h in chains]
        read = [_mm(dirs[d][1][0, 0, chunk_of[d], h], s) for (d, h), s in zip(chains, states)]
        v_new = []
        for (d, h), r in zip(chains, read):
            rs = slice(chunk_of[d] * c, (chunk_of[d] + 1) * c)
            v_new.append(dirs[d][0][0, 0, rs, h * B_HEAD_DIM:(h + 1) * B_HEAD_DIM] - r[:c])
        upd = [_mm(dirs[d][2][0, 0, chunk_of[d], h], v) for (d, h), v in zip(chains, v_new)]
        for (d, h), s, r, x in zip(chains, states, read, upd):
            cc = chunk_of[d]
            chn = d * B_HEADS + h
            dirs[d][4][0, cc * c:(cc + 1) * c, h * B_HEAD_DIM:(h + 1) * B_HEAD_DIM] = r[c:] + x[:c]
            s_ref[chn] = s * dirs[d][3][0, 0, cc, chn:chn + 1, :] + x[c:]


def _delta_scan(u, wq, akt, egl, lc):
    bsz, _, t, _ = u.shape
    n_tiles = t // SEQ_TILE
    nct = lc // SEQ_TILE
    cpt = SEQ_TILE // B_CHUNK

    def tile_of(d):
        if d == 0:
            return lambda i: i
        return lambda i: jnp.where(i < nct, nct - 1 - i, n_tiles - 1 - (i - nct))

    def specs(d):
        f = tile_of(d)
        per_chunk = lambda a: pl.BlockSpec((1, 1, cpt, B_HEADS) + a.shape[4:], lambda b, i: (b, d, f(i), 0, 0, 0))
        return [pl.BlockSpec((1, 1, SEQ_TILE, B_W), lambda b, i: (b, d, f(i), 0)),
                per_chunk(wq), per_chunk(akt),
                pl.BlockSpec((1, 1, cpt, 2 * B_HEADS, LANES), lambda b, i: (b, f(i), 0, 0, 0))]

    out_spec = lambda d: pl.BlockSpec((1, SEQ_TILE, B_W), lambda b, i: (b, tile_of(d)(i), 0))
    return pl.pallas_call(
        _delta_scan_kernel,
        grid=(bsz, n_tiles),
        in_specs=specs(0) + specs(1),
        out_specs=[out_spec(0), out_spec(1)],
        out_shape=[jax.ShapeDtypeStruct((bsz, t, B_W), F32)] * 2,
        scratch_shapes=[pltpu.VMEM((2 * B_HEADS, B_HEAD_DIM, B_HEAD_DIM), F32)],
        compiler_params=_params(("parallel", "arbitrary")),
        name="delta_scan",
    )(u, wq, akt, egl, u, wq, akt, egl)


def _residual_ffn(h, y, m, gains, wg_ref, wu_ref, wd_ref):
    h1 = h + m[2:3] * _rms(y, gains[1:2])
    u2 = (_rms(h1, gains[2:3]) * (1.0 + m[4:5]) + m[3:4]).astype(BF16)
    act = _silu(_mm(u2, wg_ref[...])) * _mm(u2, wu_ref[...])
    f = _mm(act, wd_ref[...])
    return h1 + m[5:6] * _rms(f, gains[3:4])


def _ab_post_kernel(hc_ref, hx_ref, m_ref, g_ref, oa_ref, of_ref, ob_ref, gb_ref, on_ref, wo_ref, wg_ref, wu_ref,
                    wd_ref, out_ref, *, n_ctx_tiles):
    h = _stream_tile(hc_ref, hx_ref, n_ctx_tiles)
    o = of_ref[0] + ob_ref[0]
    gate = _silu(gb_ref[0])
    heads = []
    for hh in range(B_HEADS):
        hs = slice(hh * B_HEAD_DIM, (hh + 1) * B_HEAD_DIM)
        heads.append(_rms(o[:, hs], on_ref[...]) * gate[:, hs])
    obn = jnp.concatenate(heads, axis=1)
    y = _mm(oa_ref[0], wo_ref[:A_Q_W, :]) + _mm(obn, wo_ref[A_Q_W:, :])
    out_ref[0] = _residual_ffn(h, y, m_ref[0, 0], g_ref[...], wg_ref, wu_ref, wd_ref)


def _mla_post_kernel(h_ref, m_ref, g_ref, o_ref, wo_ref, wg_ref, wu_ref, wd_ref, out_ref):
    y = _mm(o_ref[0], wo_ref[...])
    out_ref[0] = _residual_ffn(h_ref[0], y, m_ref[0, 0], g_ref[...], wg_ref, wu_ref, wd_ref)


def _stream_specs(d, n_ctx_tiles):
    tm = TOKEN_TILE
    return [pl.BlockSpec((1, tm, d), lambda b, i: (b, jnp.minimum(i, n_ctx_tiles - 1), 0)),
            pl.BlockSpec((1, tm, d), lambda b, i: (b, jnp.maximum(i - n_ctx_tiles, 0), 0))]


def _stream_tile(ctx_ref, lat_ref, n_ctx_tiles):
    return jnp.where(pl.program_id(1) < n_ctx_tiles, ctx_ref[0], lat_ref[0])


def _post_call(kernel_fn, name, h_inputs, h_specs, h_tile_off, modsel, gains, token_inputs, const_inputs, n_tokens,
               n_ctx_tiles):
    bsz, d = modsel.shape[0], modsel.shape[-1]
    tm = TOKEN_TILE
    resident = lambda a: pl.BlockSpec(a.shape, lambda *_: (0,) * a.ndim, pipeline_mode=pl.Buffered(1))
    in_specs = list(h_specs) + [
        pl.BlockSpec((1, 1, 6, d), lambda b, i: (b, ((i + h_tile_off) >= n_ctx_tiles).astype(jnp.int32), 0, 0)),
        _const_spec(gains.shape),
    ]
    in_specs += [pl.BlockSpec((1, tm, a.shape[-1]), lambda b, i: (b, i, 0)) for a in token_inputs]
    in_specs += [resident(a) if a.size * a.dtype.itemsize > (1 << 20) else _const_spec(a.shape) for a in const_inputs]
    return pl.pallas_call(
        kernel_fn,
        grid=(bsz, n_tokens // tm),
        in_specs=in_specs,
        out_specs=pl.BlockSpec((1, tm, d), lambda b, i: (b, i, 0)),
        out_shape=jax.ShapeDtypeStruct((bsz, n_tokens, d), F32),
        compiler_params=_params(("parallel", "parallel")),
        name=name,
    )(*h_inputs, modsel, gains, *token_inputs, *const_inputs)


def _mla_in_kernel(h_ref, m_ref, g_ref, win_ref, qn_ref, kvn_ref, wqt_ref, wk_ref, wvt_ref,
                   cq_ref, sq_ref, ck_ref, sk_ref, qt_ref, k_ref, vt_ref):
    m = m_ref[0, 0]
    u = _modulate_tile(h_ref[0], m[0:3], g_ref[0:1])
    p = _mm(u, win_ref[...])
    cq = _rms(p[:, :C_Q_RANK], qn_ref[...]).astype(BF16)
    ckv = _rms(p[:, C_Q_RANK:C_Q_RANK + C_KV_RANK], kvn_ref[...]).astype(BF16)
    kpe_tile = p[:, C_Q_RANK + C_KV_RANK:]
    kpe = _rope_lanes(kpe_tile, ck_ref[...], sk_ref[...], C_ROPE // 2)[:, :C_ROPE].astype(BF16)
    qt = _mm_nt(wqt_ref[...], cq)
    kn = _mm(ckv, wk_ref[...])
    vt = _mm_nt(wvt_ref[...], ckv)
    n_nope = C_HEADS * C_NOPE
    half = C_ROPE // 2
    scale = LOG2_E * C_QK ** -0.5
    cos, sin = cq_ref[...], sq_ref[...]
    ones = jnp.ones((C_V, qt.shape[1]), BF16)
    for hh in range(C_HEADS):
        x1 = qt[n_nope + hh * C_ROPE:n_nope + hh * C_ROPE + half]
        x2 = qt[n_nope + hh * C_ROPE + half:n_nope + (hh + 1) * C_ROPE]
        qh = jnp.concatenate([qt[hh * C_NOPE:(hh + 1) * C_NOPE], x1 * cos - x2 * sin, x1 * sin + x2 * cos], axis=0)
        qt_ref[0, hh] = (qh * scale).astype(BF16)
        k_ref[0, hh] = jnp.concatenate([kn[:, hh * C_NOPE:(hh + 1) * C_NOPE].astype(BF16), kpe], axis=1)
        vt_ref[0, hh] = jnp.concatenate([vt[hh * C_V:(hh + 1) * C_V].astype(BF16), ones], axis=0)


def _mla_in(hfull, modsel, gains, win, qn, kvn, wqt, wk, wvt, cos_q, sin_q, cos_k, sin_k, lc):
    bsz, t, d = hfull.shape
    tm = TOKEN_TILE
    nct = lc // tm
    tab = lambda a: pl.BlockSpec((tm, a.shape[1]), lambda b, i: (i, 0))
    tab_t = lambda a: pl.BlockSpec((a.shape[0], tm), lambda b, i: (0, i))
    return pl.pallas_call(
        _mla_in_kernel,
        grid=(bsz, t // tm),
        in_specs=[
            pl.BlockSpec((1, tm, d), lambda b, i: (b, i, 0)),
            pl.BlockSpec((1, 1, 6, d), lambda b, i: (b, (i >= nct).astype(jnp.int32), 0, 0)),
            _const_spec(gains.shape), _const_spec(win.shape), _const_spec(qn.shape), _const_spec(kvn.shape),
            _const_spec(wqt.shape), _const_spec(wk.shape), _const_spec(wvt.shape),
            tab_t(cos_q), tab_t(sin_q), tab(cos_k), tab(sin_k),
        ],
        out_specs=[
            pl.BlockSpec((1, C_HEADS, C_QK, tm), lambda b, i: (b, 0, 0, jnp.maximum(i - nct, 0))),
            pl.BlockSpec((1, C_HEADS, tm, C_QK), lambda b, i: (b, 0, i, 0)),
            pl.BlockSpec((1, C_HEADS, 2 * C_V, tm), lambda b, i: (b, 0, 0, i)),
        ],
        out_shape=[
            jax.ShapeDtypeStruct((bsz, C_HEADS, C_QK, t - lc), BF16),
            jax.ShapeDtypeStruct((bsz, C_HEADS, t, C_QK), BF16),
            jax.ShapeDtypeStruct((bsz, C_HEADS, 2 * C_V, t), BF16),
        ],
        compiler_params=_params(("parallel", "arbitrary")),
        name="mla_in_proj",
    )(hfull, modsel, gains, win, qn, kvn, wqt, wk, wvt, cos_q, sin_q, cos_k, sin_k)


def _flash_kernel(qt_ref, k_ref, vt_ref, o_ref, st_ref, *, n_kv_tiles, tk):
    heads = qt_ref.shape[1]
    tq = qt_ref.shape[3]

    ahead = FLASH_LOOKAHEAD
    assert st_ref.shape[0] == heads and 0 < ahead < heads

    def scores(j, hh):
        rows = FLASH_QK_ROWS if tk % FLASH_QK_ROWS == 0 else tk
        for part in range(tk // rows):
            start = pl.multiple_of(j * tk + part * rows, rows)
            st_ref[hh, part * rows:(part + 1) * rows] = jnp.dot(
                k_ref[0, hh, pl.ds(start, rows), :], qt_ref[0, hh], preferred_element_type=F32)

    def absorb(j, hh, m, acc):
        start = pl.multiple_of(j * tk, tk)
        m_new = jnp.maximum(m, jnp.max(st_ref[hh], axis=0, keepdims=True))
        pt = jnp.exp2(st_ref[hh] - m_new).astype(BF16)
        acc = jnp.exp2(m - m_new) * acc + jnp.dot(vt_ref[0, hh, :, pl.ds(start, tk)], pt,
                                                  preferred_element_type=F32)
        return m_new, acc

    def tile(j, state, last):
        out = list(state)
        for hh in range(heads):
            nxt = hh + ahead
            if nxt < heads:
                scores(j, nxt)
            elif not last:
                scores(j + 1, nxt - heads)
            out[hh] = absorb(j, hh, *out[hh])
        return tuple(out)

    state = tuple((jnp.full((1, tq), NEG_INF, F32), jnp.zeros((2 * C_V, tq), F32)) for _ in range(heads))
    for hh in range(ahead):
        scores(0, hh)
    n_loop = n_kv_tiles - 1
    unroll = FLASH_UNROLL if n_loop % FLASH_UNROLL == 0 else 1

    def trip(i, st):
        for k in range(unroll):
            st = tile(i * unroll + k, st, False)
        return st

    state = lax.fori_loop(0, n_loop // unroll, trip, state)
    final = tile(n_kv_tiles - 1, state, True)
    for hh in range(heads):
        acc = jnp.transpose(final[hh][1])
        o_ref[0, :, hh * C_V:(hh + 1) * C_V] = (acc[:, :C_V] / acc[:, C_V:]).astype(BF16)


def _flash(qt, k, vt):
    bsz, n_heads, _, s_len = qt.shape
    t = k.shape[2]
    tq = min(FLASH_TQ, s_len)
    tk = FLASH_TK if t % FLASH_TK == 0 else LANES
    hp = FLASH_HEADS
    return pl.pallas_call(
        functools.partial(_flash_kernel, n_kv_tiles=t // tk, tk=tk),
        grid=(bsz, n_heads // hp, s_len // tq),
        in_specs=[
            pl.BlockSpec((1, hp, C_QK, tq), lambda b, g, i: (b, g, 0, i)),
            pl.BlockSpec((1, hp, t, C_QK), lambda b, g, i: (b, g, 0, 0)),
            pl.BlockSpec((1, hp, 2 * C_V, t), lambda b, g, i: (b, g, 0, 0)),
        ],
        out_specs=pl.BlockSpec((1, tq, hp * C_V), lambda b, g, i: (b, i, g)),
        out_shape=jax.ShapeDtypeStruct((bsz, s_len, n_heads * C_V), BF16),
        scratch_shapes=[pltpu.VMEM((hp, tk, tq), F32)],
        compiler_params=_params(("parallel", "parallel", "arbitrary")),
        name="mla_flash",
    )(qt, k, vt)


def _axial_angles(rows, rot_dim):
    n_freq = rot_dim // 4
    inv_freq = ROPE_BASE ** (-jnp.arange(n_freq, dtype=F32) / n_freq)
    row = jnp.repeat(jnp.arange(rows, dtype=F32), GRID_W)
    col = jnp.tile(jnp.arange(GRID_W, dtype=F32), rows)
    return jnp.concatenate([row[:, None] * inv_freq, col[:, None] * inv_freq], axis=-1)


def _rope_tables(rows, rot_dim, lc, n_groups):
    ang = _axial_angles(rows, rot_dim)
    cos = jnp.concatenate([jnp.cos(ang), jnp.cos(ang)], axis=-1)
    sin = jnp.concatenate([-jnp.sin(ang), jnp.sin(ang)], axis=-1)
    cos = jnp.concatenate([jnp.ones((lc, rot_dim), F32), cos], axis=0)
    sin = jnp.concatenate([jnp.zeros((lc, rot_dim), F32), sin], axis=0)
    return jnp.tile(cos, (1, n_groups)), jnp.tile(sin, (1, n_groups))


def kernel(x, c, ctx, c_ctx, mod_w, mod_b, norm_gains, ffn_w_gate, ffn_w_up, ffn_w_down, ab_w_in, ab_sink, ab_conv, ab_a_log, ab_dt_bias, ab_out_norm, ab_w_out, mla_w_in, mla_q_norm, mla_kv_norm, mla_w_qb, mla_w_kvb, mla_w_out):
    bsz, s_len, d = x.shape
    lc = ctx.shape[1]
    rows = s_len // GRID_W
    assert d == D_MODEL and mod_w.shape[0] == 2
    assert lc % TOKEN_TILE == 0 and s_len % FLASH_TQ == 0 and s_len % TOKEN_TILE == 0
    n_ctx_tiles = lc // TOKEN_TILE


    n_cond = -(-(bsz + 1) // SUBLANES) * SUBLANES
    cond = jnp.concatenate([c, c_ctx[None], jnp.zeros((n_cond - bsz - 1, d), F32)], axis=0)
    mods = _modulation(cond, mod_w, mod_b)

    def mod_select(layer):
        lat = mods[layer, :bsz].reshape(bsz, 1, 6, d)
        cx = jnp.broadcast_to(mods[layer, bsz].reshape(1, 1, 6, d), (bsz, 1, 6, d))
        return jnp.concatenate([cx, lat], axis=1)

    wg = [w.astype(BF16) for w in ffn_w_gate]
    wu = [w.astype(BF16) for w in ffn_w_up]
    wd = [w.astype(BF16) for w in ffn_w_down]

    ms0 = mod_select(0)
    w_in = ab_w_in[0]
    o_q, o_k, o_v, o_b, o_g, o_dec = 0, A_Q_W, A_Q_W + A_KV_W, A_Q_W + 2 * A_KV_W, A_Q_W + 2 * A_KV_W + 3 * B_W, A_Q_W + 2 * A_KV_W + 4 * B_W
    wqk = w_in[:, o_q:o_v].astype(BF16)
    wvg = jnp.concatenate([w_in[:, o_v:o_b], w_in[:, o_dec:], jnp.zeros((d, LANES - N_GATES), F32)], axis=1).astype(BF16)
    wb = w_in[:, o_b:o_dec].astype(BF16)
    cos_a, sin_a = _rope_tables(rows, A_HEAD_DIM, lc, A_HEADS + A_KV_HEADS)
    qa, ka, va, qkvb, gb, gates = _ab_in(ctx, x, ms0, norm_gains[0], wqk, wvg, wb, cos_a, sin_a, n_ctx_tiles)

    oa = _window_attn(qa, ka, va, ab_sink[0], lc)

    zeros8 = jnp.zeros((2 * B_HEADS,), F32)
    alog = jnp.concatenate([ab_a_log[0].reshape(-1), zeros8]).reshape(1, N_GATES)
    dtb = jnp.concatenate([ab_dt_bias[0].reshape(-1), zeros8]).reshape(1, N_GATES)
    u, wq, akt, egl = _delta_prep(qkvb, gates, ab_conv[0], alog, dtb, lc)
    o_f, o_bw = _delta_scan(u, wq, akt, egl, lc)

    h1 = _post_call(functools.partial(_ab_post_kernel, n_ctx_tiles=n_ctx_tiles), "ab_post_ffn",
                    [ctx, x], _stream_specs(d, n_ctx_tiles), 0, ms0, norm_gains[0],
                    [oa, o_f, o_bw, gb],
                    [ab_out_norm[0].reshape(1, B_HEAD_DIM), ab_w_out[0].astype(BF16), wg[0], wu[0], wd[0]],
                    lc + s_len, n_ctx_tiles)

    ms1 = mod_select(1)
    win = jnp.concatenate([mla_w_in[0], jnp.zeros((d, C_IN_PAD - mla_w_in.shape[2]), F32)], axis=1).astype(BF16)
    wqb = mla_w_qb[0].reshape(C_Q_RANK, C_HEADS, C_QK)
    wqt = jnp.concatenate([wqb[:, :, :C_NOPE].reshape(C_Q_RANK, -1), wqb[:, :, C_NOPE:].reshape(C_Q_RANK, -1)], axis=1).T.astype(BF16)
    wkvb = mla_w_kvb[0].reshape(C_KV_RANK, C_HEADS, C_NOPE + C_V)
    wk = wkvb[:, :, :C_NOPE].reshape(C_KV_RANK, -1).astype(BF16)
    wvt = wkvb[:, :, C_NOPE:].reshape(C_KV_RANK, -1).T.astype(BF16)
    ang = _axial_angles(rows, C_ROPE)
    cos_q = jnp.concatenate([jnp.ones((lc, C_ROPE // 2), F32), jnp.cos(ang)], axis=0).T
    sin_q = jnp.concatenate([jnp.zeros((lc, C_ROPE // 2), F32), jnp.sin(ang)], axis=0).T
    cos_k, sin_k = _rope_tables(rows, C_ROPE, lc, 1)
    pad = LANES - C_ROPE
    cos_k = jnp.concatenate([cos_k, jnp.ones((lc + s_len, pad), F32)], axis=1)
    sin_k = jnp.concatenate([sin_k, jnp.zeros((lc + s_len, pad), F32)], axis=1)
    qt, k, vt = _mla_in(h1, ms1, norm_gains[1], win, mla_q_norm[0].reshape(1, -1), mla_kv_norm[0].reshape(1, -1),
                        wqt, wk, wvt, cos_q, sin_q, cos_k, sin_k, lc)
    o = _flash(qt, k, vt)

    h1_latent = pl.BlockSpec((1, TOKEN_TILE, d), lambda b, i: (b, i + n_ctx_tiles, 0))
    return _post_call(_mla_post_kernel, "mla_post_ffn", [h1], [h1_latent], n_ctx_tiles, ms1, norm_gains[1],
                      [o], [mla_w_out[0].astype(BF16), wg[1], wu[1], wd[1]], s_len, n_ctx_tiles)
```
